```python
import math
import jax, jax.numpy as jnp
from jax import lax
import numpy as np

D_MODEL = 1024
BATCH = 4
SEQ = 4096
DEPTH = 4
DEC_BATCH = 128
DEC_SEQ = 4
PAST_LEN = 2048
PAGE_SIZE = 128

N_A = DEPTH // 2
N_B = DEPTH - N_A
GLA_HEADS = 4
GLA_DK = D_MODEL // 2 // GLA_HEADS
GLA_DV = D_MODEL // GLA_HEADS
GLA_GATE_RANK = 16
GLA_TAU = 16.0
GLA_CHUNK = 64
GLA_IN = 2 * GLA_HEADS * GLA_DK + 2 * GLA_HEADS * GLA_DV + GLA_GATE_RANK
MOBA_HEADS = 8
HEAD_DIM = D_MODEL // MOBA_HEADS
MOBA_BLOCK = 256
MOBA_TOPK = 3
MOBA_QBLOCK = 16
MOBA_QROWS = 64
ROPE_THETA = 10000.0
PEER_HEADS = 8
PEER_NKEYS = 128
PEER_EXPERTS = PEER_NKEYS * PEER_NKEYS
PEER_DKEY = 256
PEER_TOPK = 16
PEER_TOK_BLOCK = 128
DN_ALPHA = (2.0 * DEPTH) ** 0.25
DN_BETA = (8.0 * DEPTH) ** -0.25
LN_EPS = 1e-5
POOL_FACTOR = 1.25

kernel_name = 'yoco_gla_moba_peer_step'


def _layernorm(x, g, b):
    xf = x.astype(jnp.float32)
    mu = xf.mean(-1, keepdims=True)
    var = jnp.mean(jnp.square(xf - mu), -1, keepdims=True)
    return ((xf - mu) * lax.rsqrt(var + LN_EPS) * g + b).astype(x.dtype)


def _ada(c, w, b, n):
    m = jax.nn.silu(c) @ w + b
    return [t[:, None, :] for t in jnp.split(m, n, axis=-1)]


def _rope(x, pos):
    half = x.shape[-1] // 2
    inv = ROPE_THETA ** (-jnp.arange(half, dtype=jnp.float32) / half)
    ang = pos.astype(jnp.float32)[:, None] * inv[None, :]
    cos = jnp.cos(ang)[None, :, None, :]
    sin = jnp.sin(ang)[None, :, None, :]
    x1 = x[..., :half].astype(jnp.float32)
    x2 = x[..., half:].astype(jnp.float32)
    return jnp.concatenate([x1 * cos - x2 * sin, x2 * cos + x1 * sin], -1).astype(x.dtype)


def _gla_scan(q, k, v, log_a, s0):
    B, L, H = q.shape[:3]
    C = min(GLA_CHUNK, L)
    n = -(-L // C)
    pad = n * C - L
    padf = lambda t: jnp.pad(t.astype(jnp.float32), ((0, 0), (0, pad), (0, 0), (0, 0)))
    to_chunks = lambda t: t.reshape(B, n, C, H, t.shape[-1]).transpose(1, 0, 3, 2, 4)
    qc, kc, vc, ac = [to_chunks(padf(t)) for t in (q, k, v, log_a)]
    causal = jnp.tril(jnp.ones((C, C), bool))

    def step(s, inp):
        qi, ki, vi, ai = inp
        b = jnp.cumsum(ai, axis=2)
        diff = b[:, :, :, None, :] - b[:, :, None, :, :]
        decay = jnp.exp(jnp.where(causal[:, :, None], diff, -jnp.inf))
        att = jnp.einsum('bhid,bhjd,bhijd->bhij', qi, ki, decay)
        o = (jnp.einsum('bhij,bhjv->bhiv', att, vi)
             + jnp.einsum('bhid,bhdv->bhiv', qi * jnp.exp(b), s))
        b_last = b[:, :, -1:, :]
        s_new = (jnp.exp(b_last[:, :, 0, :])[..., None] * s
                 + jnp.einsum('bhjd,bhjv->bhdv', ki * jnp.exp(b_last - b), vi))
        return s_new, o

    s_fin, o = lax.scan(step, s0.astype(jnp.float32), (qc, kc, vc, ac))
    o = o.transpose(1, 0, 3, 2, 4).reshape(B, n * C, H, v.shape[-1])[:, :L]
    return o, s_fin


def _gla_mixer(h, s0, w_in, w_g2, b_g, norm_g, w_o):
    B, L, _ = h.shape
    dq = GLA_HEADS * GLA_DK
    dv = GLA_HEADS * GLA_DV
    proj = h @ w_in
    q, k, v, r, g1 = jnp.split(proj, [dq, 2 * dq, 2 * dq + dv, 2 * dq + 2 * dv], axis=-1)
    log_a = jax.nn.log_sigmoid((g1 @ w_g2 + b_g).astype(jnp.float32)) / GLA_TAU
    q = q.reshape(B, L, GLA_HEADS, GLA_DK) * (GLA_DK ** -0.5)
    k = k.reshape(B, L, GLA_HEADS, GLA_DK)
    v = v.reshape(B, L, GLA_HEADS, GLA_DV)
    log_a = log_a.reshape(B, L, GLA_HEADS, GLA_DK)
    o, s_fin = _gla_scan(q, k, v, log_a, s0)
    o = o * lax.rsqrt(jnp.mean(jnp.square(o), -1, keepdims=True) + LN_EPS) * norm_g
    o = o.reshape(B, L, dv) * jax.nn.silu(r.astype(jnp.float32))
    return o.astype(h.dtype) @ w_o, s_fin


def _query_block(batch):
    qb = MOBA_QBLOCK
    while qb > 1 and batch * qb > MOBA_QROWS:
        qb //= 2
    return qb


def _moba_attend(q, k, v, q_pos):
    B, L, H, hd = q.shape
    T = k.shape[1]
    nb = -(-T // MOBA_BLOCK)
    pad = nb * MOBA_BLOCK - T
    kb = jnp.pad(k, ((0, 0), (0, pad), (0, 0), (0, 0))).reshape(B, nb, MOBA_BLOCK, H, hd)
    vb = jnp.pad(v, ((0, 0), (0, pad), (0, 0), (0, 0))).reshape(B, nb, MOBA_BLOCK, H, hd)
    means = kb.astype(jnp.float32).mean(axis=2)
    own = q_pos // MOBA_BLOCK
    gate = jnp.einsum('blhd,bnhd->bhln', q.astype(jnp.float32), means)
    past = jnp.arange(nb)[None, :] < own[:, None]
    gate = jnp.where(past[None, None], gate, -jnp.inf)
    k_sel = min(MOBA_TOPK, nb)
    _, top_idx = lax.top_k(gate, k_sel)
    valid_top = jnp.arange(k_sel)[None, :] < jnp.minimum(own, MOBA_TOPK)[:, None]
    sel = jnp.concatenate([top_idx.astype(jnp.int32),
                           jnp.broadcast_to(own.astype(jnp.int32)[None, None, :, None], (B, H, L, 1))], -1)
    valid = jnp.concatenate([valid_top, jnp.ones((L, 1), bool)], -1)
    S = k_sel + 1

    qb = _query_block(B)
    n = -(-L // qb)
    lp = n * qb - L
    qp = jnp.pad(q, ((0, 0), (0, lp), (0, 0), (0, 0)))
    selp = jnp.pad(sel, ((0, 0), (0, 0), (0, lp), (0, 0)), mode='edge')
    validp = jnp.pad(valid, ((0, lp), (0, 0)), mode='edge')
    posp = jnp.pad(q_pos, (0, lp), mode='edge')
    qc = qp.reshape(B, n, qb, H, hd).transpose(1, 0, 2, 3, 4)
    sc = selp.reshape(B, H, n, qb, S).transpose(2, 0, 1, 3, 4)
    vc = validp.reshape(n, qb, S)
    pc = posp.reshape(n, qb)
    bi = jnp.arange(B)[:, None, None, None]
    hi = jnp.arange(H)[None, :, None, None]
    offs = jnp.arange(MOBA_BLOCK)
    scale = hd ** -0.5

    def one(args):
        qx, sx, vx, px = args
        kg = kb[bi, sx, :, hi]
        vg = vb[bi, sx, :, hi]
        kpos = sx[..., None] * MOBA_BLOCK + offs
        mask = vx[None, None, :, :, None] & (kpos <= px[None, None, :, None, None])
        s = jnp.einsum('bqhd,bhqskd->bhqsk', qx, kg, preferred_element_type=jnp.float32) * scale
        s = jnp.where(mask, s, -jnp.inf)
        p = jax.nn.softmax(s.reshape(B, H, qb, S * MOBA_BLOCK), axis=-1).reshape(s.shape)
        o = jnp.einsum('bhqsk,bhqskd->bqhd', p, vg, preferred_element_type=jnp.float32)
        return o.astype(q.dtype)

    out = lax.map(one, (qc, sc, vc, pc))
    return out.transpose(1, 0, 2, 3, 4).reshape(B, n * qb, H, hd)[:, :L]


def _peer(h, w_q, sub_keys, u, v):
    B, L, D = h.shape
    N = B * L
    x = h.reshape(N, D)
    q = (x @ w_q).reshape(N, PEER_HEADS, 2, PEER_DKEY // 2)
    s = jnp.einsum('nhpd,pkd->nhpk', q.astype(jnp.float32), sub_keys.astype(jnp.float32))
    sv, si = lax.top_k(s, PEER_TOPK)
    cand = sv[:, :, 0, :, None] + sv[:, :, 1, None, :]
    cand_idx = si[:, :, 0, :, None] * PEER_NKEYS + si[:, :, 1, None, :]
    fv, fi = lax.top_k(cand.reshape(N, PEER_HEADS, PEER_TOPK * PEER_TOPK), PEER_TOPK)
    idx = jnp.take_along_axis(cand_idx.reshape(N, PEER_HEADS, PEER_TOPK * PEER_TOPK), fi, axis=-1)
    g = jax.nn.softmax(fv, axis=-1)
    nblk = -(-N // PEER_TOK_BLOCK)
    padn = nblk * PEER_TOK_BLOCK - N
    xp = jnp.pad(x, ((0, padn), (0, 0))).reshape(nblk, PEER_TOK_BLOCK, D)
    ip = jnp.pad(idx, ((0, padn), (0, 0), (0, 0))).reshape(nblk, PEER_TOK_BLOCK, PEER_HEADS, PEER_TOPK)
    gp = jnp.pad(g, ((0, padn), (0, 0), (0, 0))).reshape(nblk, PEER_TOK_BLOCK, PEER_HEADS, PEER_TOPK)

    def blk(args):
        xb, ib, gb = args
        a = jax.nn.gelu(jnp.einsum('td,thkd->thk', xb, u[ib], preferred_element_type=jnp.float32),
                        approximate=False)
        return jnp.einsum('thk,thkd->td', gb * a, v[ib], preferred_element_type=jnp.float32).astype(h.dtype)

    out = lax.map(blk, (xp, ip, gp)).reshape(nblk * PEER_TOK_BLOCK, D)[:N]
    return out.reshape(B, L, D)


def _trunk(x, c, gla_s0, k_past, v_past, p):
    (w_ada, b_ada, ln_g, ln_b, gla_w_in, gla_w_g2, gla_b_g, gla_norm_g, gla_w_o,
     w_ada_kv, b_ada_kv, w_kv, moba_w_q, moba_w_o, peer_w_q, peer_keys, peer_u, peer_v) = p
    B, L, _ = x.shape
    pos0 = 0 if k_past is None else k_past.shape[1]
    pos = pos0 + jnp.arange(L, dtype=jnp.int32)
    gla_fin = []
    k_new = v_new = k_all = v_all = None
    for layer in range(DEPTH):
        shift, scale, gate = _ada(c, w_ada[layer, 0], b_ada[layer, 0], 3)
        h = x * (1 + scale) + shift
        if layer < N_A:
            out, s_fin = _gla_mixer(h, gla_s0[layer], gla_w_in[layer], gla_w_g2[layer],
                                    gla_b_g[layer], gla_norm_g[layer], gla_w_o[layer])
            gla_fin.append(s_fin)
        else:
            j = layer - N_A
            q = _rope((h @ moba_w_q[j]).reshape(B, L, MOBA_HEADS, HEAD_DIM), pos)
            o = _moba_attend(q, k_all, v_all, pos)
            out = o.reshape(B, L, MOBA_HEADS * HEAD_DIM) @ moba_w_o[j]
        x = _layernorm(DN_ALPHA * x + (1 + gate) * out, ln_g[layer, 0], ln_b[layer, 0])
        shift, scale, gate = _ada(c, w_ada[layer, 1], b_ada[layer, 1], 3)
        out = _peer(x * (1 + scale) + shift, peer_w_q[layer], peer_keys[layer], peer_u[layer], peer_v[layer])
        x = _layernorm(DN_ALPHA * x + (1 + gate) * out, ln_g[layer, 1], ln_b[layer, 1])
        if layer == N_A - 1:
            shift, scale = _ada(c, w_ada_kv, b_ada_kv, 2)
            kv = ((x * (1 + scale) + shift) @ w_kv).reshape(B, L, 2 * MOBA_HEADS, HEAD_DIM)
            k_new = _rope(kv[:, :, :MOBA_HEADS], pos)
            v_new = kv[:, :, MOBA_HEADS:]
            if k_past is None:
                k_all, v_all = k_new, v_new
            else:
                k_all = jnp.concatenate([k_past.astype(k_new.dtype), k_new], axis=1)
                v_all = jnp.concatenate([v_past.astype(v_new.dtype), v_new], axis=1)
    return x, jnp.stack(gla_fin), k_new, v_new


def setup_inputs(seed: int = 0) -> dict:
    key = jax.random.key(seed)
    ks = jax.random.split(key, 28)
    f32 = jnp.float32
    nrm = lambda k, shape, s=1.0: jax.random.normal(k, shape, f32) * s
    n_pages = PAST_LEN // PAGE_SIZE
    n_used = DEC_BATCH * n_pages
    n_pool = int(math.ceil(POOL_FACTOR * n_used))
    page_table = jax.random.permutation(ks[0], n_pool)[:n_used].reshape(DEC_BATCH, n_pages).astype(jnp.int32)
    dqk = GLA_HEADS * GLA_DK
    dvt = GLA_HEADS * GLA_DV
    dm = MOBA_HEADS * HEAD_DIM
    rd = D_MODEL ** -0.5
    return {
        'x_prompt': nrm(ks[1], (BATCH, SEQ, D_MODEL)),
        'x_sample': nrm(ks[2], (DEC_BATCH, DEC_SEQ, D_MODEL)),
        'c_prompt': nrm(ks[3], (BATCH, D_MODEL)),
        'c_sample': nrm(ks[4], (DEC_BATCH, D_MODEL)),
        'state_gla': nrm(ks[5], (N_A, DEC_BATCH, GLA_HEADS, GLA_DK, GLA_DV), 0.5),
        'cache_k': nrm(ks[6], (n_pool, PAGE_SIZE, MOBA_HEADS, HEAD_DIM)),
        'cache_v': nrm(ks[7], (n_pool, PAGE_SIZE, MOBA_HEADS, HEAD_DIM)),
        'page_table': page_table,
        'w_ada': nrm(ks[8], (DEPTH, 2, D_MODEL, 3 * D_MODEL), 0.1 * rd),
        'b_ada': nrm(ks[9], (DEPTH, 2, 3 * D_MODEL), 0.01),
        'ln_g': 1.0 + nrm(ks[10], (DEPTH, 2, D_MODEL), 0.02),
        'ln_b': nrm(ks[11], (DEPTH, 2, D_MODEL), 0.02),
        'gla_w_in': nrm(ks[12], (N_A, D_MODEL, GLA_IN), rd),
        'gla_w_g2': nrm(ks[13], (N_A, GLA_GATE_RANK, dqk), GLA_GATE_RANK ** -0.5),
        'gla_b_g': nrm(ks[14], (N_A, dqk), 0.1),
        'gla_norm_g': 1.0 + nrm(ks[15], (N_A, GLA_DV), 0.02),
        'gla_w_o': nrm(ks[16], (N_A, dvt, D_MODEL), DN_BETA * dvt ** -0.5),
        'w_ada_kv': nrm(ks[17], (D_MODEL, 2 * D_MODEL), 0.1 * rd),
        'b_ada_kv': nrm(ks[18], (2 * D_MODEL,), 0.01),
        'w_kv': nrm(ks[19], (D_MODEL, 2 * dm), rd),
        'moba_w_q': nrm(ks[20], (N_B, D_MODEL, dm), rd),
        'moba_w_o': nrm(ks[21], (N_B, dm, D_MODEL), DN_BETA * dm ** -0.5),
        'peer_w_q': nrm(ks[22], (DEPTH, D_MODEL, PEER_HEADS * PEER_DKEY), rd),
        'peer_keys': nrm(ks[23], (DEPTH, 2, PEER_NKEYS, PEER_DKEY // 2), (PEER_DKEY // 2) ** -0.5),
        'peer_u': nrm(ks[24], (DEPTH, PEER_EXPERTS, D_MODEL), rd),
        'peer_v': nrm(ks[25], (DEPTH, PEER_EXPERTS, D_MODEL), DN_BETA * PEER_HEADS ** -0.5),
    }


def reference(x_prompt, x_sample, c_prompt, c_sample, state_gla, cache_k, cache_v, page_table,
              w_ada, b_ada, ln_g, ln_b, gla_w_in, gla_w_g2, gla_b_g, gla_norm_g, gla_w_o,
              w_ada_kv, b_ada_kv, w_kv, moba_w_q, moba_w_o, peer_w_q, peer_keys, peer_u, peer_v):
    params = (w_ada, b_ada, ln_g, ln_b, gla_w_in, gla_w_g2, gla_b_g, gla_norm_g, gla_w_o,
              w_ada_kv, b_ada_kv, w_kv, moba_w_q, moba_w_o, peer_w_q, peer_keys, peer_u, peer_v)
    b_p = x_prompt.shape[0]
    gla_zero = jnp.zeros((N_A, b_p, GLA_HEADS, GLA_DK, GLA_DV), jnp.float32)
    y_prompt, gla_prompt, k_prompt, v_prompt = _trunk(x_prompt, c_prompt, gla_zero, None, None, params)
    b_d, n_pages = page_table.shape
    past_len = n_pages * cache_k.shape[1]
    k_past = cache_k[page_table].reshape(b_d, past_len, MOBA_HEADS, HEAD_DIM)
    v_past = cache_v[page_table].reshape(b_d, past_len, MOBA_HEADS, HEAD_DIM)
    y_sample, gla_sample, k_sample, v_sample = _trunk(x_sample, c_sample, state_gla, k_past, v_past, params)
    return (y_prompt, y_sample, gla_prompt, gla_sample, k_prompt, v_prompt, k_sample, v_sample)
```

```python
import functools
import math

import jax
import jax.numpy as jnp
from jax import lax
from jax.experimental import pallas as pl
from jax.experimental.pallas import tpu as pltpu

F32 = jnp.float32
BF16 = jnp.bfloat16

D_MODEL = 1024
DEPTH = 4
N_A = DEPTH // 2
N_B = DEPTH - N_A
GLA_HEADS = 4
GLA_DK = D_MODEL // 2 // GLA_HEADS
GLA_DV = D_MODEL // GLA_HEADS
GLA_GATE_RANK = 16
GLA_TAU = 16.0
MOBA_HEADS = 8
HEAD_DIM = D_MODEL // MOBA_HEADS
MOBA_BLOCK = 256
MOBA_TOPK = 3
ROPE_THETA = 10000.0
PEER_HEADS = 8
PEER_NKEYS = 128
PEER_EXPERTS = PEER_NKEYS * PEER_NKEYS
PEER_DKEY = 256
PEER_TOPK = 16
DN_ALPHA = (2.0 * DEPTH) ** 0.25
LN_EPS = 1e-5

LANES = 128
VMEM_LIMIT = 56 * 1024 * 1024
NEG_INF = float("-inf")


def _cparams(*sem):
    return pltpu.CompilerParams(dimension_semantics=sem, vmem_limit_bytes=VMEM_LIMIT)


def _row_tile(n_rows, want):
    t = min(want, n_rows)
    while n_rows % t:
        t //= 2
    return t


def _mod_operand(m, seq_len, tm):
    d = m.shape[-1]
    if seq_len % tm == 0:
        per = seq_len // tm
        return m[:, None, :], (1, 1, d), (lambda i: (i // per, 0, 0))
    rows = jnp.repeat(m, seq_len, axis=0)
    return rows.reshape(rows.shape[0] // tm, tm, d), (1, tm, d), (lambda i: (i, 0, 0))


def _layernorm_rows(z, g, b):
    mu = jnp.mean(z, axis=-1, keepdims=True)
    zc = z - mu
    var = jnp.mean(zc * zc, axis=-1, keepdims=True)
    return zc * lax.rsqrt(var + LN_EPS) * g + b


def _ada_kernel(c_ref, w_ref, b_ref, o_ref):
    c = c_ref[...]
    s = c * jax.nn.sigmoid(c)
    o_ref[0] = jnp.dot(s.astype(BF16), w_ref[0].astype(BF16), preferred_element_type=F32) + b_ref[0]


def _ada_all(c, w, b):
    g, d, n = w.shape
    r = c.shape[0]
    tn = 1024
    return pl.pallas_call(
        _ada_kernel,
        grid=(g, n // tn),
        in_specs=[pl.BlockSpec((r, d), lambda gi, j: (0, 0)),
                  pl.BlockSpec((1, d, tn), lambda gi, j: (gi, 0, j)),
                  pl.BlockSpec((1, 1, tn), lambda gi, j: (gi, 0, j))],
        out_specs=pl.BlockSpec((1, r, tn), lambda gi, j: (gi, 0, j)),
        out_shape=jax.ShapeDtypeStruct((g, r, n), F32),
        compiler_params=_cparams("parallel", "parallel"),
        name="ada",
    )(c, w, b)


def _rope_cols(y, cos, sin, heads):
    outs = []
    for hh in range(heads):
        yh = y[:, hh * HEAD_DIM:(hh + 1) * HEAD_DIM]
        outs.append(yh * cos + pltpu.roll(yh, HEAD_DIM // 2, axis=1) * sin)
    return outs


def _proj_kernel(x_ref, sc_ref, sh_ref, w_ref, *rest, rope_heads, n_out):
    if rope_heads:
        cos_ref, sin_ref = rest[:2]
        outs = rest[2:]
    else:
        outs = rest
    h = x_ref[...] * (1.0 + sc_ref[0]) + sh_ref[0]
    y = jnp.dot(h.astype(BF16), w_ref[...], preferred_element_type=F32)
    if rope_heads:
        parts = _rope_cols(y, cos_ref[...], sin_ref[...], rope_heads)
        for hh, p in enumerate(parts):
            for o in outs:
                o[:, hh * HEAD_DIM:(hh + 1) * HEAD_DIM] = p.astype(o.dtype)
        lo = rope_heads * HEAD_DIM
        if lo < n_out:
            for o in outs:
                o[:, lo:] = y[:, lo:].astype(o.dtype)
    else:
        for o in outs:
            o[...] = y.astype(o.dtype)


def _proj(x, scale, shift, w_bf16, seq_len, out_dtypes, rope=None, rope_heads=0, tm_want=512):
    n, d = x.shape
    n_out = w_bf16.shape[1]
    tm = _row_tile(n, tm_want)
    sc, sc_blk, sc_idx = _mod_operand(scale, seq_len, tm)
    sh, sh_blk, sh_idx = _mod_operand(shift, seq_len, tm)
    in_specs = [pl.BlockSpec((tm, d), lambda i: (i, 0)),
                pl.BlockSpec(sc_blk, sc_idx),
                pl.BlockSpec(sh_blk, sh_idx),
                pl.BlockSpec((d, n_out), lambda i: (0, 0))]
    args = [x, sc, sh, w_bf16]
    if rope_heads:
        cos, sin = rope
        per = cos.shape[0] // tm
        in_specs += [pl.BlockSpec((tm, HEAD_DIM), lambda i: (i % per, 0)),
                     pl.BlockSpec((tm, HEAD_DIM), lambda i: (i % per, 0))]
        args += [cos, sin]
    outs = pl.pallas_call(
        functools.partial(_proj_kernel, rope_heads=rope_heads, n_out=n_out),
        grid=(n // tm,),
        in_specs=in_specs,
        out_specs=[pl.BlockSpec((tm, n_out), lambda i: (i, 0)) for _ in out_dtypes],
        out_shape=[jax.ShapeDtypeStruct((n, n_out), dt) for dt in out_dtypes],
        compiler_params=_cparams("parallel"),
        name="proj",
    )(*args)
    return outs


def _out_ln_kernel(a_ref, w_ref, x_ref, gate_ref, g_ref, b_ref, o_ref):
    y = jnp.dot(a_ref[...], w_ref[...], preferred_element_type=F32)
    z = DN_ALPHA * x_ref[...] + (1.0 + gate_ref[0]) * y
    o_ref[...] = _layernorm_rows(z, g_ref[...], b_ref[...])


def _out_ln(a_bf16, w_bf16, x, gate, ln_g, ln_b, seq_len, tm_want=512):
    n, d = x.shape
    k = a_bf16.shape[1]
    tm = _row_tile(n, tm_want)
    gt, gt_blk, gt_idx = _mod_operand(gate, seq_len, tm)
    return pl.pallas_call(
        _out_ln_kernel,
        grid=(n // tm,),
        in_specs=[pl.BlockSpec((tm, k), lambda i: (i, 0)),
                  pl.BlockSpec((k, d), lambda i: (0, 0)),
                  pl.BlockSpec((tm, d), lambda i: (i, 0)),
                  pl.BlockSpec(gt_blk, gt_idx),
                  pl.BlockSpec((1, d), lambda i: (0, 0)),
                  pl.BlockSpec((1, d), lambda i: (0, 0))],
        out_specs=pl.BlockSpec((tm, d), lambda i: (i, 0)),
        out_shape=jax.ShapeDtypeStruct((n, d), F32),
        compiler_params=_cparams("parallel"),
        name="out_ln",
    )(a_bf16, w_bf16, x, gt, ln_g.reshape(1, d), ln_b.reshape(1, d))


def _log_sigmoid(z):
    return jnp.minimum(z, 0.0) - jnp.log(1.0 + jnp.exp(-jnp.abs(z)))


def _gla_kernel(q_ref, k_ref, v_ref, r_ref, g1_ref, wg2_ref, bg_ref, ng_ref, s0_ref,
                og_ref, sfin_ref, s_scr, *, chunk, sub, seq_len, padded_len):
    c = pl.program_id(2)
    n_chunks = padded_len // chunk

    @pl.when(c == 0)
    def _():
        s_scr[...] = s0_ref[0, 0]

    z = jnp.dot(g1_ref[0], wg2_ref[...], precision=lax.Precision.HIGHEST,
                preferred_element_type=F32) + bg_ref[...]
    la = _log_sigmoid(z) * (1.0 / GLA_TAU)
    q = q_ref[0] * (GLA_DK ** -0.5)
    k = k_ref[0]
    v = v_ref[0]
    if padded_len != seq_len:
        row = c * chunk + lax.broadcasted_iota(jnp.int32, (chunk, 1), 0)
        live = row < seq_len
        la = jnp.where(live, la, 0.0)
        k = jnp.where(live, k, 0.0)
        v = jnp.where(live, v, 0.0)
    n_live = min(chunk, seq_len) if n_chunks == 1 else chunk

    ri = lax.broadcasted_iota(jnp.int32, (chunk, chunk), 0)
    ci = lax.broadcasted_iota(jnp.int32, (chunk, chunk), 1)
    tri = (ri >= ci).astype(F32)
    bc = jnp.dot(tri, la, precision=lax.Precision.HIGHEST, preferred_element_type=F32)

    s = s_scr[...]
    o = jnp.dot((q * jnp.exp(bc)).astype(BF16), s.astype(BF16), preferred_element_type=F32)
    sub_row = lax.broadcasted_iota(jnp.int32, (sub, 1), 0)
    o_parts = []
    for blk in range(chunk // sub):
        lo = blk * sub
        b_i = bc[lo:lo + sub]
        q_i = q[lo:lo + sub]
        o_i = o[lo:lo + sub]
        if blk > 0:
            b_ref = bc[lo - 1:lo]
            q_t = q_i * jnp.exp(b_i - b_ref)
            k_t = k[:lo] * jnp.exp(b_ref - bc[:lo])
            att = lax.dot_general(q_t.astype(BF16), k_t.astype(BF16), (((1,), (1,)), ((), ())),
                                  preferred_element_type=F32)
            o_i = o_i + jnp.dot(att.astype(BF16), v[:lo].astype(BF16), preferred_element_type=F32)
        for j in range(min(sub, max(n_live - lo, 0))):
            b_j = bc[lo + j:lo + j + 1]
            e = jnp.exp(jnp.where(sub_row >= j, b_i - b_j, NEG_INF))
            w_col = jnp.sum(q_i * k[lo + j:lo + j + 1] * e, axis=-1, keepdims=True)
            o_i = o_i + w_col * v[lo + j:lo + j + 1]
        o_parts.append(o_i)
    o = jnp.concatenate(o_parts, axis=0) if len(o_parts) > 1 else o_parts[0]

    o = o * lax.rsqrt(jnp.mean(o * o, axis=-1, keepdims=True) + LN_EPS) * ng_ref[...]
    r = r_ref[0]
    og_ref[0] = (o * (r * jax.nn.sigmoid(r))).astype(og_ref.dtype)

    b_last = bc[chunk - 1:chunk]
    k_d = k * jnp.exp(b_last - bc)
    decay = jnp.transpose(jnp.broadcast_to(jnp.exp(b_last), (GLA_DK, GLA_DK)))
    kv = lax.dot_general(k_d.astype(BF16), v.astype(BF16), (((0,), (0,)), ((), ())),
                         preferred_element_type=F32)
    s_new = jnp.concatenate([decay] * (GLA_DV // GLA_DK), axis=1) * s + kv
    s_scr[...] = s_new

    @pl.when(c == n_chunks - 1)
    def _():
        sfin_ref[0, 0] = s_new


def _gla_scan(proj, s0, wg2_pad, b_g, norm_g, batch, seq_len):
    cols = proj.shape[1]
    if seq_len >= 64:
        chunk, sub = 64, 16
    else:
        chunk = sub = 16
    padded = -(-seq_len // chunk) * chunk
    p3 = proj.reshape(batch, seq_len, cols)
    if padded != seq_len:
        p3 = jnp.pad(p3, ((0, 0), (0, padded - seq_len), (0, 0)))
    n_chunks = padded // chunk
    dq = GLA_HEADS * GLA_DK
    kq = dq // GLA_DK
    v0 = 2 * dq // GLA_DV
    r0 = (2 * dq + GLA_HEADS * GLA_DV) // GLA_DV
    g0 = (2 * dq + 2 * GLA_HEADS * GLA_DV) // LANES
    og, s_fin = pl.pallas_call(
        functools.partial(_gla_kernel, chunk=chunk, sub=sub, seq_len=seq_len, padded_len=padded),
        grid=(batch, GLA_HEADS, n_chunks),
        in_specs=[pl.BlockSpec((1, chunk, GLA_DK), lambda b, h, c: (b, c, h)),
                  pl.BlockSpec((1, chunk, GLA_DK), lambda b, h, c: (b, c, kq + h)),
                  pl.BlockSpec((1, chunk, GLA_DV), lambda b, h, c: (b, c, v0 + h)),
                  pl.BlockSpec((1, chunk, GLA_DV), lambda b, h, c: (b, c, r0 + h)),
                  pl.BlockSpec((1, chunk, LANES), lambda b, h, c: (b, c, g0)),
                  pl.BlockSpec((LANES, GLA_DK), lambda b, h, c: (0, h)),
                  pl.BlockSpec((1, GLA_DK), lambda b, h, c: (0, h)),
                  pl.BlockSpec((1, GLA_DV), lambda b, h, c: (0, 0)),
                  pl.BlockSpec((1, 1, GLA_DK, GLA_DV), lambda b, h, c: (b, h, 0, 0))],
        out_specs=[pl.BlockSpec((1, chunk, GLA_DV), lambda b, h, c: (b, c, h)),
                   pl.BlockSpec((1, 1, GLA_DK, GLA_DV), lambda b, h, c: (b, h, 0, 0))],
        out_shape=[jax.ShapeDtypeStruct((batch, padded, GLA_HEADS * GLA_DV), BF16),
                   jax.ShapeDtypeStruct((batch, GLA_HEADS, GLA_DK, GLA_DV), F32)],
        scratch_shapes=[pltpu.VMEM((GLA_DK, GLA_DV), F32)],
        compiler_params=_cparams("parallel", "parallel", "arbitrary"),
        name="gla_scan",
    )(p3, p3, p3, p3, p3, wg2_pad, b_g.reshape(1, dq), norm_g.reshape(1, GLA_DV), s0)
    og = og[:, :seq_len].reshape(batch * seq_len, GLA_HEADS * GLA_DV)
    return og, s_fin


def _gla_layer(x, mods, s0, w_in_pad, wg2_pad, b_g, norm_g, w_o, ln_g, ln_b, batch, seq_len):
    shift, scale, gate = mods
    (proj,) = _proj(x, scale, shift, w_in_pad, seq_len, (F32,))
    og, s_fin = _gla_scan(proj, s0, wg2_pad, b_g, norm_g, batch, seq_len)
    return _out_ln(og, w_o, x, gate, ln_g, ln_b, seq_len), s_fin


def _top_values(s, count):
    vals = []
    for r in range(count):
        m = jnp.max(s, axis=0, keepdims=True)
        vals.append(m)
        if r + 1 < count:
            s = jnp.where(s == m, NEG_INF, s)
    return vals


def _peer_select(q_scr, keys_ref, s1_scr, s2_scr, e1_scr, e2_scr, t_scr, head):
    row0 = pl.multiple_of(head * PEER_DKEY, PEER_DKEY)
    half = PEER_DKEY // 2
    s1 = jnp.dot(keys_ref[0], q_scr[pl.ds(row0, half), :],
                 precision=lax.Precision.HIGHEST, preferred_element_type=F32)
    s2 = jnp.dot(keys_ref[1], q_scr[pl.ds(row0 + half, half), :],
                 precision=lax.Precision.HIGHEST, preferred_element_type=F32)
    a = _top_values(s1, PEER_TOPK)
    b = _top_values(s2, PEER_TOPK)
    s1 = s1 - a[0]
    s2 = s2 - b[0]
    a = [v - a[0] for v in a]
    b = [v - b[0] for v in b]
    cand = [a[i] + b[j] for i in range(PEER_TOPK) for j in range(PEER_TOPK) if (i + 1) * (j + 1) <= PEER_TOPK]
    pad = (-len(cand)) % 8
    cand = jnp.concatenate(cand + [jnp.full_like(cand[0], NEG_INF)] * pad, axis=0)
    top = _top_values(cand, PEER_TOPK)
    z = top[0] * 0.0
    for v in top:
        z = z + jnp.exp(v)
    s1_scr[head] = s1
    s2_scr[head] = s2
    e1_scr[head] = jnp.exp(s1) / z
    e2_scr[head] = jnp.exp(s2)
    t_scr[head] = top[-1]


def _peer_kernel(x_ref, sc_ref, sh_ref, gate_ref, wq_ref, keys_ref, u_ref, vt_ref, lng_ref, lnb_ref,
                 o_ref, ht_scr, q_scr, s1_scr, s2_scr, e1_scr, e2_scr, t_scr, acc_scr, *, te, eb):
    j = pl.program_id(1)

    @pl.when(j == 0)
    def _():
        h = x_ref[...] * (1.0 + sc_ref[0]) + sh_ref[0]
        ht_scr[...] = jnp.transpose(h).astype(BF16)
        q_scr[...] = jnp.dot(wq_ref[...], ht_scr[...], preferred_element_type=F32)

        def head_body(head, carry):
            _peer_select(q_scr, keys_ref, s1_scr, s2_scr, e1_scr, e2_scr, t_scr, head)
            return carry
        lax.fori_loop(0, PEER_HEADS, head_body, 0)
        acc_scr[...] = jnp.zeros_like(acc_scr)

    def expert_body(e, carry):
        lo = pl.multiple_of(e * eb, eb)
        a_t = jnp.dot(u_ref[pl.ds(lo, eb), :], ht_scr[...], preferred_element_type=F32)
        key1 = (j * te + lo) // PEER_NKEYS
        w_parts = []
        for i1 in range(eb // PEER_NKEYS):
            w = None
            for head in range(PEER_HEADS):
                row = pl.ds(key1 + i1, 1)
                pair = s1_scr[head, row, :] + s2_scr[head]
                val = jnp.where(pair >= t_scr[head], e1_scr[head, row, :] * e2_scr[head], 0.0)
                w = val if w is None else w + val
            w_parts.append(w)
        w = jnp.concatenate(w_parts, axis=0) if len(w_parts) > 1 else w_parts[0]
        gelu = 0.5 * a_t * (1.0 + lax.erf(a_t * (0.5 ** 0.5)))
        p = (w * gelu).astype(BF16)
        acc_scr[...] += jnp.dot(vt_ref[:, pl.ds(lo, eb)], p, preferred_element_type=F32)
        return carry
    lax.fori_loop(0, te // eb, expert_body, 0)

    @pl.when(j == pl.num_programs(1) - 1)
    def _():
        out = jnp.transpose(acc_scr[...])
        z = DN_ALPHA * x_ref[...] + (1.0 + gate_ref[0]) * out
        o_ref[...] = _layernorm_rows(z, lng_ref[...], lnb_ref[...])


def _peer_layer(x, mods, wq_t, keys, u_bf16, v_t, ln_g, ln_b, seq_len, tm_want=512, te=1024, eb=512):
    shift, scale, gate = mods
    n, d = x.shape
    tm = _row_tile(n, tm_want)
    n_exp = u_bf16.shape[0]
    sc, sc_blk, sc_idx = _mod_operand(scale, seq_len, tm)
    sh, sh_blk, sh_idx = _mod_operand(shift, seq_len, tm)
    gt, gt_blk, gt_idx = _mod_operand(gate, seq_len, tm)
    hq = PEER_HEADS * PEER_DKEY
    row = lambda f: (lambda i, j: f(i))
    return pl.pallas_call(
        functools.partial(_peer_kernel, te=te, eb=eb),
        grid=(n // tm, n_exp // te),
        in_specs=[pl.BlockSpec((tm, d), lambda i, j: (i, 0)),
                  pl.BlockSpec(sc_blk, row(sc_idx)),
                  pl.BlockSpec(sh_blk, row(sh_idx)),
                  pl.BlockSpec(gt_blk, row(gt_idx)),
                  pl.BlockSpec((hq, d), lambda i, j: (0, 0)),
                  pl.BlockSpec((2, PEER_NKEYS, PEER_DKEY // 2), lambda i, j: (0, 0, 0)),
                  pl.BlockSpec((te, d), lambda i, j: (j, 0)),
                  pl.BlockSpec((d, te), lambda i, j: (0, j)),
                  pl.BlockSpec((1, d), lambda i, j: (0, 0)),
                  pl.BlockSpec((1, d), lambda i, j: (0, 0))],
        out_specs=pl.BlockSpec((tm, d), lambda i, j: (i, 0)),
        out_shape=jax.ShapeDtypeStruct((n, d), F32),
        scratch_shapes=[pltpu.VMEM((d, tm), BF16),
                        pltpu.VMEM((hq, tm), F32),
                        pltpu.VMEM((PEER_HEADS, PEER_NKEYS, tm), F32),
                        pltpu.VMEM((PEER_HEADS, PEER_NKEYS, tm), F32),
                        pltpu.VMEM((PEER_HEADS, PEER_NKEYS, tm), F32),
                        pltpu.VMEM((PEER_HEADS, PEER_NKEYS, tm), F32),
                        pltpu.VMEM((PEER_HEADS, 1, tm), F32),
                        pltpu.VMEM((d, tm), F32)],
        compiler_params=_cparams("parallel", "arbitrary"),
        name="peer",
    )(x, sc, sh, gt, wq_t, keys, u_bf16, v_t, ln_g.reshape(1, d), ln_b.reshape(1, d))


def _block_mean_kernel(*refs, n_parts):
    o_ref = refs[-1]
    total = None
    rows = 0
    for p_ref in refs[-1 - n_parts:-1]:
        part = jnp.sum(p_ref[0].astype(F32), axis=0, keepdims=True)
        total = part if total is None else total + part
        rows += p_ref.shape[1]
    o_ref[0] = total * (1.0 / rows)


def _block_means_dense(k, batch, seq_len):
    w = k.shape[1]
    nb = seq_len // MOBA_BLOCK
    out = pl.pallas_call(
        functools.partial(_block_mean_kernel, n_parts=1),
        grid=(batch * nb,),
        in_specs=[pl.BlockSpec((1, MOBA_BLOCK, w), lambda i: (i, 0, 0))],
        out_specs=pl.BlockSpec((1, 1, w), lambda i: (i, 0, 0)),
        out_shape=jax.ShapeDtypeStruct((batch * nb, 1, w), F32),
        compiler_params=_cparams("parallel"),
        name="block_means",
    )(k.reshape(batch * nb, MOBA_BLOCK, w))
    return out.reshape(batch, nb, w)


def _block_means_paged(cache, page_table_flat, batch, pages_per_seq):
    _, page, w = cache.shape
    ppb = MOBA_BLOCK // page
    nb = pages_per_seq // ppb
    page_spec = lambda part: pl.BlockSpec(
        (1, page, w), lambda b, j, pt: (pt[b * pages_per_seq + j * ppb + part], 0, 0))
    out = pl.pallas_call(
        functools.partial(_block_mean_kernel, n_parts=ppb),
        grid_spec=pltpu.PrefetchScalarGridSpec(
            num_scalar_prefetch=1,
            grid=(batch, nb),
            in_specs=[page_spec(p) for p in range(ppb)],
            out_specs=pl.BlockSpec((1, 1, w), lambda b, j, pt: (b * nb + j, 0, 0))),
        out_shape=jax.ShapeDtypeStruct((batch * nb, 1, w), F32),
        compiler_params=_cparams("parallel", "parallel"),
        name="block_means_paged",
    )(page_table_flat, *([cache] * ppb))
    return out.reshape(batch, nb, w)


def _moba_prompt_kernel(qi_ref, kv_ref, q_ref, k_ref, v_ref, mean_ref, o_ref,
                        m_scr, l_scr, acc_scr, idx_scr, *, n_blocks):
    t = pl.program_id(1)
    qi = qi_ref[t]
    kvi = kv_ref[t]
    own = kvi == qi
    blk = MOBA_BLOCK
    scale = HEAD_DIM ** -0.5
    row = lax.broadcasted_iota(jnp.int32, (blk, blk), 0)
    col = lax.broadcasted_iota(jnp.int32, (blk, blk), 1)
    causal = col <= row
    gcol = lax.broadcasted_iota(jnp.int32, (blk, n_blocks), 1)

    for h in range(MOBA_HEADS):
        cs = slice(h * HEAD_DIM, (h + 1) * HEAD_DIM)
        q = q_ref[0, :, cs]

        @pl.when(own)
        def _():
            g = lax.dot_general(q.astype(F32), mean_ref[0, :, cs], (((1,), (1,)), ((), ())),
                                precision=lax.Precision.HIGHEST, preferred_element_type=F32)
            g = jnp.where(gcol < qi, g, NEG_INF)
            for r in range(MOBA_TOPK):
                best = jnp.max(g, axis=-1, keepdims=True)
                pick = jnp.min(jnp.where(g == best, gcol, n_blocks), axis=-1, keepdims=True)
                idx_scr[h, r] = jnp.where(r < qi, pick, -1)
                g = jnp.where(gcol == pick, NEG_INF, g)
            m_scr[h] = jnp.full((blk, 1), NEG_INF, F32)
            l_scr[h] = jnp.zeros((blk, 1), F32)
            acc_scr[h] = jnp.zeros((blk, HEAD_DIM), F32)

        s = lax.dot_general(q, k_ref[0, :, cs], (((1,), (1,)), ((), ())),
                            preferred_element_type=F32) * scale
        chosen = idx_scr[h, 0] == kvi
        for r in range(1, MOBA_TOPK):
            chosen = jnp.logical_or(chosen, idx_scr[h, r] == kvi)
        s = jnp.where(own, jnp.where(causal, s, NEG_INF), jnp.where(chosen, s, NEG_INF))
        m_old = m_scr[h]
        m_new = jnp.maximum(m_old, jnp.max(s, axis=-1, keepdims=True))
        alpha = jnp.exp(m_old - m_new)
        p = jnp.exp(s - m_new)
        l_scr[h] = alpha * l_scr[h] + jnp.sum(p, axis=-1, keepdims=True)
        acc_scr[h] = alpha * acc_scr[h] + jnp.dot(p.astype(BF16), v_ref[0, :, cs],
                                                  preferred_element_type=F32)
        m_scr[h] = m_new

        @pl.when(kvi == 0)
        def _():
            o_ref[0, :, cs] = (acc_scr[h] / l_scr[h]).astype(o_ref.dtype)


def _moba_prompt(q, k, v, means, batch, seq_len):
    w = q.shape[1]
    nb = seq_len // MOBA_BLOCK
    qi_tab = jnp.asarray([qi for qi in range(nb) for _ in range(qi + 1)], jnp.int32)
    kv_tab = jnp.asarray([qi - s for qi in range(nb) for s in range(qi + 1)], jnp.int32)
    r3 = lambda a: a.reshape(batch, seq_len, w)
    out = pl.pallas_call(
        functools.partial(_moba_prompt_kernel, n_blocks=nb),
        grid_spec=pltpu.PrefetchScalarGridSpec(
            num_scalar_prefetch=2,
            grid=(batch, int(qi_tab.shape[0])),
            in_specs=[pl.BlockSpec((1, MOBA_BLOCK, w), lambda b, t, qt, kt: (b, qt[t], 0)),
                      pl.BlockSpec((1, MOBA_BLOCK, w), lambda b, t, qt, kt: (b, kt[t], 0)),
                      pl.BlockSpec((1, MOBA_BLOCK, w), lambda b, t, qt, kt: (b, kt[t], 0)),
                      pl.BlockSpec((1, nb, w), lambda b, t, qt, kt: (b, 0, 0))],
            out_specs=pl.BlockSpec((1, MOBA_BLOCK, w), lambda b, t, qt, kt: (b, qt[t], 0)),
            scratch_shapes=[pltpu.VMEM((MOBA_HEADS, MOBA_BLOCK, 1), F32),
                            pltpu.VMEM((MOBA_HEADS, MOBA_BLOCK, 1), F32),
                            pltpu.VMEM((MOBA_HEADS, MOBA_BLOCK, HEAD_DIM), F32),
                            pltpu.VMEM((MOBA_HEADS, MOBA_TOPK, MOBA_BLOCK, 1), jnp.int32)]),
        out_shape=jax.ShapeDtypeStruct((batch, seq_len, w), BF16),
        compiler_params=_cparams("parallel", "arbitrary"),
        name="moba_prompt",
    )(qi_tab, kv_tab, r3(q), r3(k), r3(v), means)
    return out.reshape(batch * seq_len, w)


def _row_to_col(row_vec):
    n = row_vec.shape[1]
    eye = lax.broadcasted_iota(jnp.int32, (n, n), 0) == lax.broadcasted_iota(jnp.int32, (n, n), 1)
    return jnp.sum(jnp.where(eye, jnp.broadcast_to(row_vec, (n, n)), 0.0), axis=-1, keepdims=True)


def _moba_decode_kernel(pt_ref, q_ref, kn_ref, vn_ref, mean_ref, *rest, n_new, n_past, ppb):
    k_pages = rest[:ppb]
    v_pages = rest[ppb:2 * ppb]
    o_ref, qbd_scr, m_scr, l_scr, acc_scr, idx_scr = rest[2 * ppb:]
    j = pl.program_id(1)
    rows = q_ref.shape[1]
    cols = MOBA_HEADS * rows
    w = MOBA_HEADS * HEAD_DIM
    scale = HEAD_DIM ** -0.5

    @pl.when(j == 0)
    def _():
        q_t = jnp.transpose(q_ref[0])
        tiled = jnp.concatenate([q_t] * MOBA_HEADS, axis=1)
        row_head = lax.broadcasted_iota(jnp.int32, (w, cols), 0) // HEAD_DIM
        col_head = lax.broadcasted_iota(jnp.int32, (w, cols), 1) // rows
        qbd = jnp.where(row_head == col_head, tiled, 0.0)
        qbd_scr[...] = qbd.astype(BF16)

        g = jnp.dot(mean_ref[0], qbd, precision=lax.Precision.HIGHEST, preferred_element_type=F32)
        grow = lax.broadcasted_iota(jnp.int32, g.shape, 0)
        for r in range(MOBA_TOPK):
            best = jnp.max(g, axis=0, keepdims=True)
            pick = jnp.min(jnp.where(g == best, grow, n_past), axis=0, keepdims=True)
            idx_scr[r] = pick if r < n_past else jnp.full_like(pick, -1)
            g = jnp.where(grow == pick, NEG_INF, g)

        s = jnp.dot(kn_ref[0].astype(BF16), qbd_scr[...], preferred_element_type=F32) * scale
        key_t = lax.broadcasted_iota(jnp.int32, (rows, cols), 0)
        qry_r = lax.broadcasted_iota(jnp.int32, (rows, cols), 1) % rows
        s = jnp.where(jnp.logical_and(key_t <= qry_r, key_t < n_new), s, NEG_INF)
        m = jnp.max(s, axis=0, keepdims=True)
        p = jnp.exp(s - m)
        m_scr[...] = m
        l_scr[...] = jnp.sum(p, axis=0, keepdims=True)
        acc_scr[...] = lax.dot_general(p.astype(BF16), vn_ref[0].astype(BF16), (((0,), (0,)), ((), ())),
                                       preferred_element_type=F32)

    k_blk = jnp.concatenate([r[0] for r in k_pages], axis=0).astype(BF16)
    v_blk = jnp.concatenate([r[0] for r in v_pages], axis=0).astype(BF16)
    s = jnp.dot(k_blk, qbd_scr[...], preferred_element_type=F32) * scale
    chosen = idx_scr[0] == j
    for r in range(1, MOBA_TOPK):
        chosen = jnp.logical_or(chosen, idx_scr[r] == j)
    s = jnp.where(chosen, s, NEG_INF)
    m_old = m_scr[...]
    m_new = jnp.maximum(m_old, jnp.max(s, axis=0, keepdims=True))
    alpha = jnp.exp(m_old - m_new)
    p = jnp.exp(s - m_new)
    l_scr[...] = alpha * l_scr[...] + jnp.sum(p, axis=0, keepdims=True)
    pv = lax.dot_general(p.astype(BF16), v_blk, (((0,), (0,)), ((), ())), preferred_element_type=F32)
    acc_scr[...] = _row_to_col(alpha) * acc_scr[...] + pv
    m_scr[...] = m_new

    @pl.when(j == n_past - 1)
    def _():
        inv_l = 1.0 / _row_to_col(l_scr[...])
        for h in range(MOBA_HEADS):
            o_ref[0, :, h * HEAD_DIM:(h + 1) * HEAD_DIM] = (
                acc_scr[h * rows:(h + 1) * rows, h * HEAD_DIM:(h + 1) * HEAD_DIM]
                * inv_l[h * rows:(h + 1) * rows]).astype(o_ref.dtype)


def _moba_decode(q, k_new, v_new, means, cache_k, cache_v, page_table_flat, batch, n_new, pages_per_seq):
    w = q.shape[1]
    page = cache_k.shape[1]
    ppb = MOBA_BLOCK // page
    n_past = pages_per_seq // ppb
    rows = 8
    pad = lambda a: jnp.pad(a.reshape(batch, n_new, w), ((0, 0), (0, rows - n_new), (0, 0)))
    cols = MOBA_HEADS * rows
    seq_spec = pl.BlockSpec((1, rows, w), lambda b, j, pt: (b, 0, 0))
    page_spec = lambda part: pl.BlockSpec(
        (1, page, w), lambda b, j, pt: (pt[b * pages_per_seq + j * ppb + part], 0, 0))
    out = pl.pallas_call(
        functools.partial(_moba_decode_kernel, n_new=n_new, n_past=n_past, ppb=ppb),
        grid_spec=pltpu.PrefetchScalarGridSpec(
            num_scalar_prefetch=1,
            grid=(batch, n_past),
            in_specs=[seq_spec, seq_spec, seq_spec,
                      pl.BlockSpec((1, n_past, w), lambda b, j, pt: (b, 0, 0))]
                     + [page_spec(p) for p in range(ppb)] * 2,
            out_specs=seq_spec,
            scratch_shapes=[pltpu.VMEM((w, cols), BF16),
                            pltpu.VMEM((1, cols), F32),
                            pltpu.VMEM((1, cols), F32),
                            pltpu.VMEM((cols, w), F32),
                            pltpu.VMEM((MOBA_TOPK, 1, cols), jnp.int32)]),
        out_shape=jax.ShapeDtypeStruct((batch, rows, w), F32),
        compiler_params=_cparams("parallel", "arbitrary"),
        name="moba_decode",
    )(page_table_flat, pad(q), pad(k_new), pad(v_new), means,
      *([cache_k] * ppb), *([cache_v] * ppb))
    return out[:, :n_new].reshape(batch * n_new, w)


def _rope_tables(pos):
    half = HEAD_DIM // 2
    inv = ROPE_THETA ** (-jnp.arange(half, dtype=F32) / half)
    ang = pos.astype(F32)[:, None] * inv[None, :]
    cos, sin = jnp.cos(ang), jnp.sin(ang)
    return jnp.concatenate([cos, cos], axis=-1), jnp.concatenate([-sin, sin], axis=-1)


def _trunk(x, mods, mods_kv, s0, past, p, batch, seq_len):
    pos0 = 0 if past is None else past[3] * past[0].shape[1]
    pos = pos0 + jnp.arange(seq_len, dtype=jnp.int32)
    cos, sin = _rope_tables(pos)
    n = batch * seq_len
    if seq_len % _row_tile(n, 512):
        cos, sin = jnp.tile(cos, (batch, 1)), jnp.tile(sin, (batch, 1))
    gla_fin = []
    k_new = v_new = None
    for layer in range(DEPTH):
        shift, scale, gate = mods[layer][0]
        if layer < N_A:
            x, s_fin = _gla_layer(x, mods[layer][0], s0[layer], p["gla_w_in"][layer], p["gla_w_g2"][layer],
                                  p["gla_b_g"][layer], p["gla_norm_g"][layer], p["gla_w_o"][layer],
                                  p["ln_g"][layer, 0], p["ln_b"][layer, 0], batch, seq_len)
            gla_fin.append(s_fin)
        else:
            jb = layer - N_A
            if past is None:
                (q,) = _proj(x, scale, shift, p["moba_w_q"][jb], seq_len, (BF16,), rope=(cos, sin),
                             rope_heads=MOBA_HEADS)
                o = _moba_prompt(q, k_bf16, v_bf16, means, batch, seq_len)
            else:
                (q,) = _proj(x, scale, shift, p["moba_w_q"][jb], seq_len, (F32,), rope=(cos, sin),
                             rope_heads=MOBA_HEADS)
                o = _moba_decode(q, k_new, v_new, means, past[0], past[1], past[2], batch, seq_len,
                                 past[3]).astype(BF16)
            x = _out_ln(o, p["moba_w_o"][jb], x, gate, p["ln_g"][layer, 0], p["ln_b"][layer, 0], seq_len)
        x = _peer_layer(x, mods[layer][1], p["peer_w_q"][layer], p["peer_keys"][layer], p["peer_u"][layer],
                        p["peer_v"][layer], p["ln_g"][layer, 1], p["ln_b"][layer, 1], seq_len)
        if layer == N_A - 1:
            kv_shift, kv_scale = mods_kv
            kv, kv_bf16 = _proj(x, kv_scale, kv_shift, p["w_kv"], seq_len, (F32, BF16), rope=(cos, sin),
                                rope_heads=MOBA_HEADS)
            dm = MOBA_HEADS * HEAD_DIM
            k_new, v_new = kv[:, :dm], kv[:, dm:]
            if past is None:
                k_bf16, v_bf16 = kv_bf16[:, :dm], kv_bf16[:, dm:]
                means = _block_means_dense(k_new, batch, seq_len)
            else:
                means = _block_means_paged(past[0], past[2], batch, past[3])
    return x, jnp.stack(gla_fin), k_new, v_new


def kernel(x_prompt, x_sample, c_prompt, c_sample, state_gla, cache_k, cache_v, page_table,
           w_ada, b_ada, ln_g, ln_b, gla_w_in, gla_w_g2, gla_b_g, gla_norm_g, gla_w_o,
           w_ada_kv, b_ada_kv, w_kv, moba_w_q, moba_w_o, peer_w_q, peer_keys, peer_u, peer_v):
    b_p, l_p, d = x_prompt.shape
    b_s, l_s, _ = x_sample.shape
    gla_cols = -(-gla_w_in.shape[-1] // LANES) * LANES
    p = {
        "ln_g": ln_g, "ln_b": ln_b,
        "gla_w_in": jnp.pad(gla_w_in, ((0, 0), (0, 0), (0, gla_cols - gla_w_in.shape[-1]))).astype(BF16),
        "gla_w_g2": jnp.pad(gla_w_g2, ((0, 0), (0, LANES - GLA_GATE_RANK), (0, 0))),
        "gla_b_g": gla_b_g, "gla_norm_g": gla_norm_g,
        "gla_w_o": gla_w_o.astype(BF16),
        "w_kv": w_kv.astype(BF16),
        "moba_w_q": moba_w_q.astype(BF16), "moba_w_o": moba_w_o.astype(BF16),
        "peer_w_q": jnp.swapaxes(peer_w_q, 1, 2).astype(BF16),
        "peer_keys": peer_keys,
        "peer_u": peer_u.astype(BF16),
        "peer_v": jnp.swapaxes(peer_v, 1, 2).astype(BF16),
    }
    c_all = jnp.concatenate([c_prompt, c_sample], axis=0)
    rows = c_all.shape[0]
    c_all = jnp.pad(c_all, ((0, (-rows) % 8), (0, 0)))
    m_all = _ada_all(c_all, w_ada.reshape(DEPTH * 2, d, 3 * d), b_ada.reshape(DEPTH * 2, 1, 3 * d))
    m_kv = _ada_all(c_all, w_ada_kv[None], b_ada_kv[None, None])[0]

    def group_mods(lo, hi):
        mods = [[tuple(m_all[layer * 2 + sub, lo:hi, i * d:(i + 1) * d] for i in range(3))
                 for sub in range(2)] for layer in range(DEPTH)]
        return mods, (m_kv[lo:hi, :d], m_kv[lo:hi, d:])

    mods_p, kv_p = group_mods(0, b_p)
    mods_s, kv_s = group_mods(b_p, b_p + b_s)

    zero_state = jnp.zeros((N_A, b_p) + state_gla.shape[2:], F32)
    y_p, gla_p, k_p, v_p = _trunk(x_prompt.reshape(b_p * l_p, d), mods_p, kv_p, zero_state, None, p, b_p, l_p)

    pool, page = cache_k.shape[:2]
    past = (cache_k.reshape(pool, page, MOBA_HEADS * HEAD_DIM), cache_v.reshape(pool, page, MOBA_HEADS * HEAD_DIM),
            page_table.reshape(-1).astype(jnp.int32), page_table.shape[1])
    y_s, gla_s, k_s, v_s = _trunk(x_sample.reshape(b_s * l_s, d), mods_s, kv_s, state_gla, past, p, b_s, l_s)

    heads = lambda a, b, l: a.reshape(b, l, MOBA_HEADS, HEAD_DIM)
    return (y_p.reshape(b_p, l_p, d), y_s.reshape(b_s, l_s, d), gla_p, gla_s,
            heads(k_p, b_p, l_p), heads(v_p, b_p, l_p), heads(k_s, b_s, l_s), heads(v_s, b_s, l_s))
```

```python
import functools
import math

import jax
import jax.numpy as jnp
from jax import lax
from jax.experimental import pallas as pl
from jax.experimental.pallas import tpu as pltpu

F32 = jnp.float32
BF16 = jnp.bfloat16

D_MODEL = 1024
DEPTH = 4
N_A = DEPTH // 2
N_B = DEPTH - N_A
GLA_HEADS = 4
GLA_DK = D_MODEL // 2 // GLA_HEADS
GLA_DV = D_MODEL // GLA_HEADS
GLA_GATE_RANK = 16
GLA_TAU = 16.0
MOBA_HEADS = 8
HEAD_DIM = D_MODEL // MOBA_HEADS
MOBA_BLOCK = 256
MOBA_TOPK = 3
ROPE_THETA = 10000.0
PEER_HEADS = 8
PEER_NKEYS = 128
PEER_EXPERTS = PEER_NKEYS * PEER_NKEYS
PEER_DKEY = 256
PEER_TOPK = 16
DN_ALPHA = (2.0 * DEPTH) ** 0.25
LN_EPS = 1e-5

LANES = 128
VMEM_LIMIT = 56 * 1024 * 1024
NEG_INF = float("-inf")


def _cparams(*sem):
    return pltpu.CompilerParams(dimension_semantics=sem, vmem_limit_bytes=VMEM_LIMIT)


def _row_tile(n_rows, want):
    t = min(want, n_rows)
    while n_rows % t:
        t //= 2
    return t


def _mod_operand(m, seq_len, tm):
    d = m.shape[-1]
    if seq_len % tm == 0:
        per = seq_len // tm
        return m[:, None, :], (1, 1, d), (lambda i: (i // per, 0, 0))
    rows = jnp.repeat(m, seq_len, axis=0)
    return rows.reshape(rows.shape[0] // tm, tm, d), (1, tm, d), (lambda i: (i, 0, 0))


def _layernorm_rows(z, g, b):
    mu = jnp.mean(z, axis=-1, keepdims=True)
    zc = z - mu
    var = jnp.mean(zc * zc, axis=-1, keepdims=True)
    return zc * lax.rsqrt(var + LN_EPS) * g + b


def _ada_kernel(c_ref, w_ref, b_ref, o_ref):
    c = c_ref[...]
    s = c * jax.nn.sigmoid(c)
    o_ref[0] = jnp.dot(s.astype(BF16), w_ref[0].astype(BF16), preferred_element_type=F32) + b_ref[0]


def _ada_all(c, w, b):
    g, d, n = w.shape
    r = c.shape[0]
    tn = 1024
    return pl.pallas_call(
        _ada_kernel,
        grid=(g, n // tn),
        in_specs=[pl.BlockSpec((r, d), lambda gi, j: (0, 0)),
                  pl.BlockSpec((1, d, tn), lambda gi, j: (gi, 0, j)),
                  pl.BlockSpec((1, 1, tn), lambda gi, j: (gi, 0, j))],
        out_specs=pl.BlockSpec((1, r, tn), lambda gi, j: (gi, 0, j)),
        out_shape=jax.ShapeDtypeStruct((g, r, n), F32),
        compiler_params=_cparams("parallel", "parallel"),
        name="ada",
    )(c, w, b)


def _rope_cols(y, cos, sin, heads):
    outs = []
    for hh in range(heads):
        yh = y[:, hh * HEAD_DIM:(hh + 1) * HEAD_DIM]
        outs.append(yh * cos + pltpu.roll(yh, HEAD_DIM // 2, axis=1) * sin)
    return outs


def _proj_kernel(x_ref, sc_ref, sh_ref, w_ref, *rest, rope_heads, n_out, out_scale):
    if rope_heads:
        cos_ref, sin_ref = rest[:2]
        outs = rest[2:]
    else:
        outs = rest
    h = x_ref[...] * (1.0 + sc_ref[0]) + sh_ref[0]
    y = jnp.dot(h.astype(BF16), w_ref[...], preferred_element_type=F32)
    if out_scale != 1.0:
        y = y * out_scale
    if rope_heads:
        parts = _rope_cols(y, cos_ref[...], sin_ref[...], rope_heads)
        for hh, p in enumerate(parts):
            for o in outs:
                o[:, hh * HEAD_DIM:(hh + 1) * HEAD_DIM] = p.astype(o.dtype)
        lo = rope_heads * HEAD_DIM
        if lo < n_out:
            for o in outs:
                o[:, lo:] = y[:, lo:].astype(o.dtype)
    else:
        for o in outs:
            o[...] = y.astype(o.dtype)


def _proj(x, scale, shift, w_bf16, seq_len, out_dtypes, rope=None, rope_heads=0, out_scale=1.0, tm_want=512):
    n, d = x.shape
    n_out = w_bf16.shape[1]
    tm = _row_tile(n, tm_want)
    sc, sc_blk, sc_idx = _mod_operand(scale, seq_len, tm)
    sh, sh_blk, sh_idx = _mod_operand(shift, seq_len, tm)
    in_specs = [pl.BlockSpec((tm, d), lambda i: (i, 0)),
                pl.BlockSpec(sc_blk, sc_idx),
                pl.BlockSpec(sh_blk, sh_idx),
                pl.BlockSpec((d, n_out), lambda i: (0, 0))]
    args = [x, sc, sh, w_bf16]
    if rope_heads:
        cos, sin = rope
        per = cos.shape[0] // tm
        in_specs += [pl.BlockSpec((tm, HEAD_DIM), lambda i: (i % per, 0)),
                     pl.BlockSpec((tm, HEAD_DIM), lambda i: (i % per, 0))]
        args += [cos, sin]
    outs = pl.pallas_call(
        functools.partial(_proj_kernel, rope_heads=rope_heads, n_out=n_out, out_scale=out_scale),
        grid=(n // tm,),
        in_specs=in_specs,
        out_specs=[pl.BlockSpec((tm, n_out), lambda i: (i, 0)) for _ in out_dtypes],
        out_shape=[jax.ShapeDtypeStruct((n, n_out), dt) for dt in out_dtypes],
        compiler_params=_cparams("parallel"),
        name="proj",
    )(*args)
    return outs


def _out_ln_kernel(a_ref, w_ref, x_ref, gate_ref, g_ref, b_ref, o_ref):
    y = jnp.dot(a_ref[...], w_ref[...], preferred_element_type=F32)
    z = DN_ALPHA * x_ref[...] + (1.0 + gate_ref[0]) * y
    o_ref[...] = _layernorm_rows(z, g_ref[...], b_ref[...])


def _out_ln(a_bf16, w_bf16, x, gate, ln_g, ln_b, seq_len, tm_want=512):
    n, d = x.shape
    k = a_bf16.shape[1]
    tm = _row_tile(n, tm_want)
    gt, gt_blk, gt_idx = _mod_operand(gate, seq_len, tm)
    return pl.pallas_call(
        _out_ln_kernel,
        grid=(n // tm,),
        in_specs=[pl.BlockSpec((tm, k), lambda i: (i, 0)),
                  pl.BlockSpec((k, d), lambda i: (0, 0)),
                  pl.BlockSpec((tm, d), lambda i: (i, 0)),
                  pl.BlockSpec(gt_blk, gt_idx),
                  pl.BlockSpec((1, d), lambda i: (0, 0)),
                  pl.BlockSpec((1, d), lambda i: (0, 0))],
        out_specs=pl.BlockSpec((tm, d), lambda i: (i, 0)),
        out_shape=jax.ShapeDtypeStruct((n, d), F32),
        compiler_params=_cparams("parallel"),
        name="out_ln",
    )(a_bf16, w_bf16, x, gt, ln_g.reshape(1, d), ln_b.reshape(1, d))


def _log_sigmoid(z):
    return jnp.minimum(z, 0.0) - jnp.log(1.0 + jnp.exp(-jnp.abs(z)))


def _gla_kernel(q_ref, k_ref, v_ref, r_ref, g1_ref, wg2_ref, bg_ref, ng_ref, s0_ref,
                og_ref, sfin_ref, s_scr, *, chunk, sub, seq_len, padded_len):
    c = pl.program_id(2)
    n_chunks = padded_len // chunk

    @pl.when(c == 0)
    def _():
        s_scr[...] = s0_ref[0, 0]

    z = jnp.dot(g1_ref[0], wg2_ref[...], precision=lax.Precision.HIGHEST,
                preferred_element_type=F32) + bg_ref[...]
    la = _log_sigmoid(z) * (1.0 / GLA_TAU)
    q = q_ref[0] * (GLA_DK ** -0.5)
    k = k_ref[0]
    v = v_ref[0]
    if padded_len != seq_len:
        row = c * chunk + lax.broadcasted_iota(jnp.int32, (chunk, 1), 0)
        live = row < seq_len
        la = jnp.where(live, la, 0.0)
        k = jnp.where(live, k, 0.0)
        v = jnp.where(live, v, 0.0)
    n_live = min(chunk, seq_len) if n_chunks == 1 else chunk

    ri = lax.broadcasted_iota(jnp.int32, (chunk, chunk), 0)
    ci = lax.broadcasted_iota(jnp.int32, (chunk, chunk), 1)
    tri = (ri >= ci).astype(F32)
    bc = jnp.dot(tri, la, precision=lax.Precision.HIGHEST, preferred_element_type=F32)

    s = s_scr[...]
    o = jnp.dot((q * jnp.exp(bc)).astype(BF16), s.astype(BF16), preferred_element_type=F32)
    sub_row = lax.broadcasted_iota(jnp.int32, (sub, 1), 0)
    o_parts = []
    for blk in range(chunk // sub):
        lo = blk * sub
        b_i = bc[lo:lo + sub]
        q_i = q[lo:lo + sub]
        o_i = o[lo:lo + sub]
        if blk > 0:
            b_ref = bc[lo - 1:lo]
            q_t = q_i * jnp.exp(b_i - b_ref)
            k_t = k[:lo] * jnp.exp(b_ref - bc[:lo])
            att = lax.dot_general(q_t.astype(BF16), k_t.astype(BF16), (((1,), (1,)), ((), ())),
                                  preferred_element_type=F32)
            o_i = o_i + jnp.dot(att.astype(BF16), v[:lo].astype(BF16), preferred_element_type=F32)
        for j in range(min(sub, max(n_live - lo, 0))):
            b_j = bc[lo + j:lo + j + 1]
            e = jnp.exp(jnp.where(sub_row >= j, b_i - b_j, NEG_INF))
            w_col = jnp.sum(q_i * k[lo + j:lo + j + 1] * e, axis=-1, keepdims=True)
            o_i = o_i + w_col * v[lo + j:lo + j + 1]
        o_parts.append(o_i)
    o = jnp.concatenate(o_parts, axis=0) if len(o_parts) > 1 else o_parts[0]

    o = o * lax.rsqrt(jnp.mean(o * o, axis=-1, keepdims=True) + LN_EPS) * ng_ref[...]
    r = r_ref[0]
    og_ref[0] = (o * (r * jax.nn.sigmoid(r))).astype(og_ref.dtype)

    b_last = bc[chunk - 1:chunk]
    k_d = k * jnp.exp(b_last - bc)
    decay = jnp.transpose(jnp.broadcast_to(jnp.exp(b_last), (GLA_DK, GLA_DK)))
    kv = lax.dot_general(k_d.astype(BF16), v.astype(BF16), (((0,), (0,)), ((), ())),
                         preferred_element_type=F32)
    s_new = jnp.concatenate([decay] * (GLA_DV // GLA_DK), axis=1) * s + kv
    s_scr[...] = s_new

    @pl.when(c == n_chunks - 1)
    def _():
        sfin_ref[0, 0] = s_new


def _gla_scan(proj, s0, wg2_pad, b_g, norm_g, batch, seq_len):
    cols = proj.shape[1]
    if seq_len >= 64:
        chunk, sub = 64, 16
    else:
        chunk = sub = 16
    padded = -(-seq_len // chunk) * chunk
    p3 = proj.reshape(batch, seq_len, cols)
    if padded != seq_len:
        p3 = jnp.pad(p3, ((0, 0), (0, padded - seq_len), (0, 0)))
    n_chunks = padded // chunk
    dq = GLA_HEADS * GLA_DK
    kq = dq // GLA_DK
    v0 = 2 * dq // GLA_DV
    r0 = (2 * dq + GLA_HEADS * GLA_DV) // GLA_DV
    g0 = (2 * dq + 2 * GLA_HEADS * GLA_DV) // LANES
    og, s_fin = pl.pallas_call(
        functools.partial(_gla_kernel, chunk=chunk, sub=sub, seq_len=seq_len, padded_len=padded),
        grid=(batch, GLA_HEADS, n_chunks),
        in_specs=[pl.BlockSpec((1, chunk, GLA_DK), lambda b, h, c: (b, c, h)),
                  pl.BlockSpec((1, chunk, GLA_DK), lambda b, h, c: (b, c, kq + h)),
                  pl.BlockSpec((1, chunk, GLA_DV), lambda b, h, c: (b, c, v0 + h)),
                  pl.BlockSpec((1, chunk, GLA_DV), lambda b, h, c: (b, c, r0 + h)),
                  pl.BlockSpec((1, chunk, LANES), lambda b, h, c: (b, c, g0)),
                  pl.BlockSpec((LANES, GLA_DK), lambda b, h, c: (0, h)),
                  pl.BlockSpec((1, GLA_DK), lambda b, h, c: (0, h)),
                  pl.BlockSpec((1, GLA_DV), lambda b, h, c: (0, 0)),
                  pl.BlockSpec((1, 1, GLA_DK, GLA_DV), lambda b, h, c: (b, h, 0, 0))],
        out_specs=[pl.BlockSpec((1, chunk, GLA_DV), lambda b, h, c: (b, c, h)),
                   pl.BlockSpec((1, 1, GLA_DK, GLA_DV), lambda b, h, c: (b, h, 0, 0))],
        out_shape=[jax.ShapeDtypeStruct((batch, padded, GLA_HEADS * GLA_DV), BF16),
                   jax.ShapeDtypeStruct((batch, GLA_HEADS, GLA_DK, GLA_DV), F32)],
        scratch_shapes=[pltpu.VMEM((GLA_DK, GLA_DV), F32)],
        compiler_params=_cparams("parallel", "parallel", "arbitrary"),
        name="gla_scan",
    )(p3, p3, p3, p3, p3, wg2_pad, b_g.reshape(1, dq), norm_g.reshape(1, GLA_DV), s0)
    og = og[:, :seq_len].reshape(batch * seq_len, GLA_HEADS * GLA_DV)
    return og, s_fin


def _gla_layer(x, mods, s0, w_in_pad, wg2_pad, b_g, norm_g, w_o, ln_g, ln_b, batch, seq_len):
    shift, scale, gate = mods
    (proj,) = _proj(x, scale, shift, w_in_pad, seq_len, (F32,))
    og, s_fin = _gla_scan(proj, s0, wg2_pad, b_g, norm_g, batch, seq_len)
    return _out_ln(og, w_o, x, gate, ln_g, ln_b, seq_len), s_fin


def _top_values(s, count, want_rank=False):
    vals = []
    rank = jnp.full(s.shape, float(count), F32) if want_rank else None
    for r in range(count):
        m = jnp.max(s, axis=0, keepdims=True)
        vals.append(m)
        hit = s == m
        if want_rank:
            rank = jnp.where(hit, float(r), rank)
        if r + 1 < count:
            s = jnp.where(hit, NEG_INF, s)
    return (vals, rank) if want_rank else vals


def _peer_select(q_scr, keys_ref, c_scr, e1_scr, r2_scr, e2_scr, head):
    row0 = pl.multiple_of(head * PEER_DKEY, PEER_DKEY)
    half = PEER_DKEY // 2
    s1_all = jnp.dot(keys_ref[0], q_scr[pl.ds(row0, half), :],
                     precision=lax.Precision.HIGHEST, preferred_element_type=F32)
    s2_all = jnp.dot(keys_ref[1], q_scr[pl.ds(row0 + half, half), :],
                     precision=lax.Precision.HIGHEST, preferred_element_type=F32)
    width = min(LANES, s1_all.shape[1])
    for g in range(s1_all.shape[1] // width):
        cols = slice(g * width, (g + 1) * width)
        _peer_select_group(s1_all[:, cols], s2_all[:, cols], c_scr, e1_scr, r2_scr, e2_scr, head, cols)


def _peer_select_group(s1, s2, c_scr, e1_scr, r2_scr, e2_scr, head, cols):
    a = _top_values(s1, PEER_TOPK)
    b, rank2 = _top_values(s2, PEER_TOPK, want_rank=True)
    s1 = s1 - a[0]
    s2 = s2 - b[0]
    a = [v - a[0] for v in a]
    b = [v - b[0] for v in b]
    cand = [a[i] + b[j] for i in range(PEER_TOPK) for j in range(PEER_TOPK) if (i + 1) * (j + 1) <= PEER_TOPK]
    pad = (-len(cand)) % 8
    cand = jnp.concatenate(cand + [jnp.full_like(cand[0], NEG_INF)] * pad, axis=0)
    top = _top_values(cand, PEER_TOPK)
    cut = top[-1]
    z = jnp.zeros_like(cut)
    for v in top:
        z = z + jnp.exp(v)
    count = jnp.zeros_like(s1)
    for bj in b:
        count = count + jnp.where(s1 + bj >= cut, 1.0, 0.0)
    c_scr[head, :, cols] = count
    e1_scr[head, :, cols] = jnp.exp(s1) / z
    r2_scr[head, :, cols] = rank2.astype(BF16)
    e2_scr[head, :, cols] = jnp.exp(s2).astype(BF16)


def _peer_kernel(x_ref, sc_ref, sh_ref, gate_ref, wq_ref, keys_ref, u_ref, vt_ref, lng_ref, lnb_ref,
                 o_ref, ht_scr, q_scr, c_scr, e1_scr, r2_scr, e2_scr, acc_scr, a_scr, p_scr, *, te, eb):
    j = pl.program_id(1)

    @pl.when(j == 0)
    def _():
        h = x_ref[...] * (1.0 + sc_ref[0]) + sh_ref[0]
        ht_scr[...] = jnp.transpose(h).astype(BF16)
        q_scr[...] = jnp.dot(wq_ref[...], ht_scr[...], preferred_element_type=F32)

        def head_body(head, carry):
            _peer_select(q_scr, keys_ref, c_scr, e1_scr, r2_scr, e2_scr, head)
            return carry
        lax.fori_loop(0, PEER_HEADS, head_body, 0)
        acc_scr[...] = jnp.zeros_like(acc_scr)

    def activations(e):
        a_scr[e % 2] = jnp.dot(u_ref[e * eb:(e + 1) * eb, :], ht_scr[...],
                               preferred_element_type=F32)

    def weigh(e):
        key1 = (j * te + e * eb) // PEER_NKEYS
        for i1 in range(eb // PEER_NKEYS):
            rows = slice(i1 * PEER_NKEYS, (i1 + 1) * PEER_NKEYS)
            w = None
            for head in range(PEER_HEADS):
                row = pl.ds(key1 + i1, 1)
                keep = r2_scr[head] < c_scr[head, row, :].astype(BF16)
                val = jnp.where(keep, e2_scr[head] * e1_scr[head, row, :].astype(BF16), jnp.zeros((), BF16))
                w = val if w is None else w + val
            a_t = a_scr[e % 2, rows, :]
            gelu = 0.5 * a_t * (1.0 + lax.erf(a_t * (0.5 ** 0.5)))
            p_scr[e % 2, rows, :] = w * gelu.astype(BF16)

    def combine(e):
        acc_scr[...] += jnp.dot(vt_ref[:, e * eb:(e + 1) * eb], p_scr[e % 2],
                                preferred_element_type=F32)

    n_blocks = te // eb
    activations(0)
    for e in range(n_blocks):
        if e + 1 < n_blocks:
            activations(e + 1)
        weigh(e)
        combine(e)

    @pl.when(j == pl.num_programs(1) - 1)
    def _():
        out = jnp.transpose(acc_scr[...])
        z = DN_ALPHA * x_ref[...] + (1.0 + gate_ref[0]) * out
        o_ref[...] = _layernorm_rows(z, lng_ref[...], lnb_ref[...])


def _peer_layer(x, mods, wq_t, keys, u_bf16, v_t, ln_g, ln_b, seq_len, tm_want=512, te=2048, eb=512):
    shift, scale, gate = mods
    n, d = x.shape
    tm = _row_tile(n, tm_want)
    n_exp = u_bf16.shape[0]
    sc, sc_blk, sc_idx = _mod_operand(scale, seq_len, tm)
    sh, sh_blk, sh_idx = _mod_operand(shift, seq_len, tm)
    gt, gt_blk, gt_idx = _mod_operand(gate, seq_len, tm)
    hq = PEER_HEADS * PEER_DKEY
    row = lambda f: (lambda i, j: f(i))
    return pl.pallas_call(
        functools.partial(_peer_kernel, te=te, eb=eb),
        grid=(n // tm, n_exp // te),
        in_specs=[pl.BlockSpec((tm, d), lambda i, j: (i, 0)),
                  pl.BlockSpec(sc_blk, row(sc_idx)),
                  pl.BlockSpec(sh_blk, row(sh_idx)),
                  pl.BlockSpec(gt_blk, row(gt_idx)),
                  pl.BlockSpec((hq, d), lambda i, j: (0, 0)),
                  pl.BlockSpec((2, PEER_NKEYS, PEER_DKEY // 2), lambda i, j: (0, 0, 0)),
                  pl.BlockSpec((te, d), lambda i, j: (j, 0)),
                  pl.BlockSpec((d, te), lambda i, j: (0, j)),
                  pl.BlockSpec((1, d), lambda i, j: (0, 0)),
                  pl.BlockSpec((1, d), lambda i, j: (0, 0))],
        out_specs=pl.BlockSpec((tm, d), lambda i, j: (i, 0)),
        out_shape=jax.ShapeDtypeStruct((n, d), F32),
        scratch_shapes=[pltpu.VMEM((d, tm), BF16),
                        pltpu.VMEM((hq, tm), F32),
                        pltpu.VMEM((PEER_HEADS, PEER_NKEYS, tm), F32),
                        pltpu.VMEM((PEER_HEADS, PEER_NKEYS, tm), F32),
                        pltpu.VMEM((PEER_HEADS, PEER_NKEYS, tm), BF16),
                        pltpu.VMEM((PEER_HEADS, PEER_NKEYS, tm), BF16),
                        pltpu.VMEM((d, tm), F32),
                        pltpu.VMEM((2, eb, tm), F32),
                        pltpu.VMEM((2, eb, tm), BF16)],
        compiler_params=_cparams("parallel", "arbitrary"),
        name="peer",
    )(x, sc, sh, gt, wq_t, keys, u_bf16, v_t, ln_g.reshape(1, d), ln_b.reshape(1, d))


def _block_mean_kernel(*refs, n_parts):
    o_ref = refs[-1]
    total = None
    rows = 0
    for p_ref in refs[-1 - n_parts:-1]:
        part = jnp.sum(p_ref[0].astype(F32), axis=0, keepdims=True)
        total = part if total is None else total + part
        rows += p_ref.shape[1]
    o_ref[0] = total * (1.0 / rows)


def _block_means_dense(k, batch, seq_len):
    w = k.shape[1]
    nb = seq_len // MOBA_BLOCK
    out = pl.pallas_call(
        functools.partial(_block_mean_kernel, n_parts=1),
        grid=(batch * nb,),
        in_specs=[pl.BlockSpec((1, MOBA_BLOCK, w), lambda i: (i, 0, 0))],
        out_specs=pl.BlockSpec((1, 1, w), lambda i: (i, 0, 0)),
        out_shape=jax.ShapeDtypeStruct((batch * nb, 1, w), F32),
        compiler_params=_cparams("parallel"),
        name="block_means",
    )(k.reshape(batch * nb, MOBA_BLOCK, w))
    return out.reshape(batch, nb, w)


def _paged_mean_kernel(pt_ref, *refs):
    o_ref = refs[-1]
    total = None
    rows = 0
    for p_ref in refs[:-1]:
        part = jnp.sum(p_ref[0], axis=0)
        total = part if total is None else total + part
        rows += p_ref.shape[1]
    o_ref[0] = total * (1.0 / rows)


def _block_means_paged(cache, page_table_flat, batch, pages_per_seq):
    _, page, heads, hd = cache.shape
    ppb = MOBA_BLOCK // page
    nb = pages_per_seq // ppb
    page_spec = lambda part: pl.BlockSpec(
        (1, page, heads, hd), lambda b, j, pt: (pt[b * pages_per_seq + j * ppb + part], 0, 0, 0))
    out = pl.pallas_call(
        _paged_mean_kernel,
        grid_spec=pltpu.PrefetchScalarGridSpec(
            num_scalar_prefetch=1,
            grid=(batch, nb),
            in_specs=[page_spec(p) for p in range(ppb)],
            out_specs=pl.BlockSpec((1, heads, hd), lambda b, j, pt: (b * nb + j, 0, 0))),
        out_shape=jax.ShapeDtypeStruct((batch * nb, heads, hd), F32),
        compiler_params=_cparams("parallel", "parallel"),
        name="block_means_paged",
    )(page_table_flat, *([cache] * ppb))
    return out.reshape(batch, nb * heads, hd)


def _moba_prompt_kernel(qi_ref, kv_ref, q_ref, k_ref, v_ref, mean_ref, o_ref,
                        m_scr, acc_scr, qa_scr):
    t = pl.program_id(1)
    qi = qi_ref[t]
    kvi = kv_ref[t]
    own = kvi == qi
    blk = MOBA_BLOCK
    masked = 1e30
    row = lax.broadcasted_iota(jnp.int32, (blk, blk), 0)
    col = lax.broadcasted_iota(jnp.int32, (blk, blk), 1)
    causal_bias = jnp.where(own, jnp.where(col <= row, 0.0, NEG_INF), 0.0)
    gcol_i = lax.broadcasted_iota(jnp.int32, (blk, LANES), 1)
    head_cols = [slice(h * HEAD_DIM, (h + 1) * HEAD_DIM) for h in range(MOBA_HEADS)]

    @pl.when(own)
    def _():
        gcol = gcol_i.astype(F32)
        n_past = qi.astype(F32)
        for h, cs in enumerate(head_cols):
            g = lax.dot_general(q_ref[0, :, cs].astype(F32), mean_ref[0, :, cs], (((1,), (1,)), ((), ())),
                                precision=lax.Precision.HIGHEST, preferred_element_type=F32)
            g = jnp.where(gcol < n_past, g, NEG_INF)
            bias = jnp.where(gcol == n_past, 0.0, -masked)
            for r in range(MOBA_TOPK):
                best = jnp.max(g, axis=-1, keepdims=True)
                pick = jnp.min(jnp.where(g == best, gcol, float(LANES)), axis=-1, keepdims=True)
                hit = gcol == pick
                bias = jnp.where(r < qi, jnp.where(hit, 0.0, bias), bias)
                g = jnp.where(hit, NEG_INF, g)
            qa_scr[h] = jnp.concatenate([q_ref[0, :, cs], bias.astype(BF16)], axis=1)
            m_scr[h] = jnp.full((blk, LANES), NEG_INF, F32)
            acc_scr[h] = jnp.zeros((blk, 2 * HEAD_DIM), F32)

    onehot = jnp.where(gcol_i == kvi, 1.0, 0.0).astype(BF16)
    ones = jnp.ones((blk, HEAD_DIM), BF16)
    twice = lambda a: jnp.concatenate([a, a], axis=1)
    for h, cs in enumerate(head_cols):
        k_aug = jnp.concatenate([k_ref[0, :, cs], onehot], axis=1)
        s = lax.dot_general(qa_scr[h], k_aug, (((1,), (1,)), ((), ())), preferred_element_type=F32) + causal_bias
        m_old = m_scr[h]
        m_new = jnp.maximum(m_old, jnp.max(s, axis=-1, keepdims=True))
        alpha = jnp.exp(m_old - m_new)
        p = jnp.exp(s - twice(m_new))
        v_aug = jnp.concatenate([v_ref[0, :, cs], ones], axis=1)
        acc_scr[h] = twice(alpha) * acc_scr[h] + jnp.dot(p.astype(BF16), v_aug, preferred_element_type=F32)
        m_scr[h] = m_new

    @pl.when(kvi == 0)
    def _():
        for h, cs in enumerate(head_cols):
            acc = acc_scr[h]
            o_ref[0, :, cs] = (acc[:, :HEAD_DIM] / acc[:, HEAD_DIM:]).astype(o_ref.dtype)


def _moba_prompt(q, k, v, means, batch, seq_len):
    w = q.shape[1]
    nb = seq_len // MOBA_BLOCK
    assert nb <= LANES
    means = jnp.pad(means, ((0, 0), (0, LANES - nb), (0, 0)))
    qi_tab = jnp.asarray([qi for qi in range(nb) for _ in range(qi + 1)], jnp.int32)
    kv_tab = jnp.asarray([qi - s for qi in range(nb) for s in range(qi + 1)], jnp.int32)
    r3 = lambda a: a.reshape(batch, seq_len, w)
    out = pl.pallas_call(
        _moba_prompt_kernel,
        grid_spec=pltpu.PrefetchScalarGridSpec(
            num_scalar_prefetch=2,
            grid=(batch, int(qi_tab.shape[0])),
            in_specs=[pl.BlockSpec((1, MOBA_BLOCK, w), lambda b, t, qt, kt: (b, qt[t], 0)),
                      pl.BlockSpec((1, MOBA_BLOCK, w), lambda b, t, qt, kt: (b, kt[t], 0)),
                      pl.BlockSpec((1, MOBA_BLOCK, w), lambda b, t, qt, kt: (b, kt[t], 0)),
                      pl.BlockSpec((1, LANES, w), lambda b, t, qt, kt: (b, 0, 0))],
            out_specs=pl.BlockSpec((1, MOBA_BLOCK, w), lambda b, t, qt, kt: (b, qt[t], 0)),
            scratch_shapes=[pltpu.VMEM((MOBA_HEADS, MOBA_BLOCK, LANES), F32),
                            pltpu.VMEM((MOBA_HEADS, MOBA_BLOCK, 2 * HEAD_DIM), F32),
                            pltpu.VMEM((MOBA_HEADS, MOBA_BLOCK, HEAD_DIM + LANES), BF16)]),
        out_shape=jax.ShapeDtypeStruct((batch, seq_len, w), BF16),
        compiler_params=_cparams("parallel", "arbitrary"),
        name="moba_prompt",
    )(qi_tab, kv_tab, r3(q), r3(k), r3(v), means)
    return out.reshape(batch * seq_len, w)


def _moba_decode_kernel(pt_ref, q_ref, kn_ref, vn_ref, mean_ref, *rest, n_new, n_past, ppb):
    k_pages = rest[:ppb]
    v_pages = rest[ppb:2 * ppb]
    o_ref, m_scr, l_scr, acc_scr, idx_scr = rest[2 * ppb:]
    j = pl.program_id(1)
    nq = MOBA_HEADS * n_new
    head_bits = MOBA_HEADS.bit_length() - 1
    new_bits = n_new.bit_length() - 1
    q = q_ref[0]
    qb = q.astype(BF16)

    def same_head(n_cols):
        lane_head = lax.broadcasted_iota(jnp.int32, (nq, n_cols), 1) & (MOBA_HEADS - 1)
        row_head = lax.broadcasted_iota(jnp.int32, (nq, n_cols), 0) >> new_bits
        return lane_head == row_head

    def scores(keys):
        return lax.dot_general(qb, keys, (((1,), (1,)), ((), ())), preferred_element_type=F32)

    @pl.when(j == 0)
    def _():
        g = lax.dot_general(q, mean_ref[0], (((1,), (1,)), ((), ())),
                            precision=lax.Precision.HIGHEST, preferred_element_type=F32)
        lane = lax.broadcasted_iota(jnp.int32, g.shape, 1)
        g = jnp.where(same_head(g.shape[1]), g, NEG_INF)
        for r in range(MOBA_TOPK):
            best = jnp.max(g, axis=-1, keepdims=True)
            pick = jnp.min(jnp.where(g == best, lane, g.shape[1]), axis=-1, keepdims=True)
            idx_scr[r] = (pick >> head_bits) if r < n_past else jnp.full_like(pick, -1)
            g = jnp.where(lane == pick, NEG_INF, g)

        s = scores(kn_ref[0].astype(BF16))
        key_t = lax.broadcasted_iota(jnp.int32, s.shape, 1) >> head_bits
        qry_r = lax.broadcasted_iota(jnp.int32, s.shape, 0) & (n_new - 1)
        s = jnp.where(same_head(s.shape[1]), jnp.where(key_t <= qry_r, s, NEG_INF), NEG_INF)
        m = jnp.max(s, axis=-1, keepdims=True)
        p = jnp.exp(s - m)
        m_scr[...] = m
        l_scr[...] = jnp.sum(p, axis=-1, keepdims=True)
        acc_scr[...] = jnp.dot(p.astype(BF16), vn_ref[0].astype(BF16), preferred_element_type=F32)

    page_rows = k_pages[0].shape[1] * MOBA_HEADS
    head_bias = jnp.where(same_head(page_rows), 0.0, NEG_INF)
    pages_per_block = MOBA_BLOCK // k_pages[0].shape[1]
    s_pages = []
    for i, k_ref in enumerate(k_pages):
        block = j * (ppb // pages_per_block) + i // pages_per_block
        chosen = idx_scr[0] == block
        for r in range(1, MOBA_TOPK):
            chosen = jnp.logical_or(chosen, idx_scr[r] == block)
        k2 = k_ref[0].reshape(page_rows, HEAD_DIM).astype(BF16)
        s_pages.append(scores(k2) + head_bias + jnp.where(chosen, 0.0, NEG_INF))
    m_old = m_scr[...]
    m_new = m_old
    for s in s_pages:
        m_new = jnp.maximum(m_new, jnp.max(s, axis=-1, keepdims=True))
    alpha = jnp.exp(m_old - m_new)
    l_new = alpha * l_scr[...]
    acc = alpha * acc_scr[...]
    for s, v_ref in zip(s_pages, v_pages):
        p = jnp.exp(s - m_new)
        l_new = l_new + jnp.sum(p, axis=-1, keepdims=True)
        v2 = v_ref[0].reshape(page_rows, HEAD_DIM).astype(BF16)
        acc = acc + jnp.dot(p.astype(BF16), v2, preferred_element_type=F32)
    m_scr[...] = m_new
    l_scr[...] = l_new
    acc_scr[...] = acc

    @pl.when(j == pl.num_programs(1) - 1)
    def _():
        o_ref[0] = acc / l_new


def _moba_decode(q, k_new, v_new, means, cache_k, cache_v, page_table_flat, batch, n_new, pages_per_seq):
    _, page, heads, hd = cache_k.shape
    assert heads & (heads - 1) == 0 and n_new & (n_new - 1) == 0
    n_past = pages_per_seq * page // MOBA_BLOCK
    blocks_per_step = 2 if n_past % 2 == 0 else 1
    ppb = (MOBA_BLOCK // page) * blocks_per_step
    nq = heads * n_new
    q_hq = q.reshape(batch, n_new, heads, hd).transpose(0, 2, 1, 3).reshape(batch, nq, hd)
    kv_rows = lambda a: a.reshape(batch, nq, hd)
    seq_spec = pl.BlockSpec((1, nq, hd), lambda b, j, pt: (b, 0, 0))
    page_spec = lambda part: pl.BlockSpec(
        (1, page, heads, hd), lambda b, j, pt: (pt[b * pages_per_seq + j * ppb + part], 0, 0, 0))
    out = pl.pallas_call(
        functools.partial(_moba_decode_kernel, n_new=n_new, n_past=n_past, ppb=ppb),
        grid_spec=pltpu.PrefetchScalarGridSpec(
            num_scalar_prefetch=1,
            grid=(batch, n_past // blocks_per_step),
            in_specs=[seq_spec, seq_spec, seq_spec,
                      pl.BlockSpec((1, n_past * heads, hd), lambda b, j, pt: (b, 0, 0))]
                     + [page_spec(p) for p in range(ppb)] * 2,
            out_specs=seq_spec,
            scratch_shapes=[pltpu.VMEM((nq, 1), F32),
                            pltpu.VMEM((nq, 1), F32),
                            pltpu.VMEM((nq, hd), F32),
                            pltpu.VMEM((MOBA_TOPK, nq, 1), jnp.int32)]),
        out_shape=jax.ShapeDtypeStruct((batch, nq, hd), F32),
        compiler_params=_cparams("parallel", "arbitrary"),
        name="moba_decode",
    )(page_table_flat, q_hq, kv_rows(k_new), kv_rows(v_new), means,
      *([cache_k] * ppb), *([cache_v] * ppb))
    return out.reshape(batch, heads, n_new, hd).transpose(0, 2, 1, 3).reshape(batch * n_new, heads * hd)


def _rope_tables(pos):
    half = HEAD_DIM // 2
    inv = ROPE_THETA ** (-jnp.arange(half, dtype=F32) / half)
    ang = pos.astype(F32)[:, None] * inv[None, :]
    cos, sin = jnp.cos(ang), jnp.sin(ang)
    return jnp.concatenate([cos, cos], axis=-1), jnp.concatenate([-sin, sin], axis=-1)


def _trunk(x, mods, mods_kv, s0, past, p, batch, seq_len):
    pos0 = 0 if past is None else past[3] * past[0].shape[1]
    pos = pos0 + jnp.arange(seq_len, dtype=jnp.int32)
    cos, sin = _rope_tables(pos)
    n = batch * seq_len
    if seq_len % _row_tile(n, 512):
        cos, sin = jnp.tile(cos, (batch, 1)), jnp.tile(sin, (batch, 1))
    gla_fin = []
    k_new = v_new = None
    for layer in range(DEPTH):
        shift, scale, gate = mods[layer][0]
        if layer < N_A:
            x, s_fin = _gla_layer(x, mods[layer][0], s0[layer], p["gla_w_in"][layer], p["gla_w_g2"][layer],
                                  p["gla_b_g"][layer], p["gla_norm_g"][layer], p["gla_w_o"][layer],
                                  p["ln_g"][layer, 0], p["ln_b"][layer, 0], batch, seq_len)
            gla_fin.append(s_fin)
        else:
            jb = layer - N_A
            if past is None:
                (q,) = _proj(x, scale, shift, p["moba_w_q"][jb], seq_len, (BF16,), rope=(cos, sin),
                             rope_heads=MOBA_HEADS, out_scale=HEAD_DIM ** -0.5)
                o = _moba_prompt(q, k_bf16, v_bf16, means, batch, seq_len)
            else:
                (q,) = _proj(x, scale, shift, p["moba_w_q"][jb], seq_len, (F32,), rope=(cos, sin),
                             rope_heads=MOBA_HEADS, out_scale=HEAD_DIM ** -0.5)
                o = _moba_decode(q, k_new, v_new, means, past[0], past[1], past[2], batch, seq_len,
                                 past[3]).astype(BF16)
            x = _out_ln(o, p["moba_w_o"][jb], x, gate, p["ln_g"][layer, 0], p["ln_b"][layer, 0], seq_len)
        x = _peer_layer(x, mods[layer][1], p["peer_w_q"][layer], p["peer_keys"][layer], p["peer_u"][layer],
                        p["peer_v"][layer], p["ln_g"][layer, 1], p["ln_b"][layer, 1], seq_len)
        if layer == N_A - 1:
            kv_shift, kv_scale = mods_kv
            kv, kv_bf16 = _proj(x, kv_scale, kv_shift, p["w_kv"], seq_len, (F32, BF16), rope=(cos, sin),
                                rope_heads=MOBA_HEADS)
            dm = MOBA_HEADS * HEAD_DIM
            k_new, v_new = kv[:, :dm], kv[:, dm:]
            if past is None:
                k_bf16, v_bf16 = kv_bf16[:, :dm], kv_bf16[:, dm:]
                means = _block_means_dense(k_new, batch, seq_len)
            else:
                means = _block_means_paged(past[0], past[2], batch, past[3])
    return x, jnp.stack(gla_fin), k_new, v_new


def kernel(x_prompt, x_sample, c_prompt, c_sample, state_gla, cache_k, cache_v, page_table,
           w_ada, b_ada, ln_g, ln_b, gla_w_in, gla_w_g2, gla_b_g, gla_norm_g, gla_w_o,
           w_ada_kv, b_ada_kv, w_kv, moba_w_q, moba_w_o, peer_w_q, peer_keys, peer_u, peer_v):
    b_p, l_p, d = x_prompt.shape
    b_s, l_s, _ = x_sample.shape
    gla_cols = -(-gla_w_in.shape[-1] // LANES) * LANES
    p = {
        "ln_g": ln_g, "ln_b": ln_b,
        "gla_w_in": jnp.pad(gla_w_in, ((0, 0), (0, 0), (0, gla_cols - gla_w_in.shape[-1]))).astype(BF16),
        "gla_w_g2": jnp.pad(gla_w_g2, ((0, 0), (0, LANES - GLA_GATE_RANK), (0, 0))),
        "gla_b_g": gla_b_g, "gla_norm_g": gla_norm_g,
        "gla_w_o": gla_w_o.astype(BF16),
        "w_kv": w_kv.astype(BF16),
        "moba_w_q": moba_w_q.astype(BF16), "moba_w_o": moba_w_o.astype(BF16),
        "peer_w_q": jnp.swapaxes(peer_w_q, 1, 2).astype(BF16),
        "peer_keys": peer_keys,
        "peer_u": peer_u.astype(BF16),
        "peer_v": jnp.swapaxes(peer_v, 1, 2).astype(BF16),
    }
    c_all = jnp.concatenate([c_prompt, c_sample], axis=0)
    rows = c_all.shape[0]
    c_all = jnp.pad(c_all, ((0, (-rows) % 8), (0, 0)))
    m_all = _ada_all(c_all, w_ada.reshape(DEPTH * 2, d, 3 * d), b_ada.reshape(DEPTH * 2, 1, 3 * d))
    m_kv = _ada_all(c_all, w_ada_kv[None], b_ada_kv[None, None])[0]

    def group_mods(lo, hi):
        mods = [[tuple(m_all[layer * 2 + sub, lo:hi, i * d:(i + 1) * d] for i in range(3))
                 for sub in range(2)] for layer in range(DEPTH)]
        return mods, (m_kv[lo:hi, :d], m_kv[lo:hi, d:])

    mods_p, kv_p = group_mods(0, b_p)
    mods_s, kv_s = group_mods(b_p, b_p + b_s)

    zero_state = jnp.zeros((N_A, b_p) + state_gla.shape[2:], F32)
    y_p, gla_p, k_p, v_p = _trunk(x_prompt.reshape(b_p * l_p, d), mods_p, kv_p, zero_state, None, p, b_p, l_p)

    past = (cache_k, cache_v, page_table.reshape(-1).astype(jnp.int32), page_table.shape[1])
    y_s, gla_s, k_s, v_s = _trunk(x_sample.reshape(b_s * l_s, d), mods_s, kv_s, state_gla, past, p, b_s, l_s)

    heads = lambda a, b, l: a.reshape(b, l, MOBA_HEADS, HEAD_DIM)
    return (y_p.reshape(b_p, l_p, d), y_s.reshape(b_s, l_s, d), gla_p, gla_s,
            heads(k_p, b_p, l_p), heads(v_p, b_p, l_p), heads(k_s, b_s, l_s), heads(v_s, b_s, l_s))
```

```python
import functools
import math

import jax
import jax.numpy as jnp
from jax import lax
from jax.experimental import pallas as pl
from jax.experimental.pallas import tpu as pltpu

F32 = jnp.float32
BF16 = jnp.bfloat16

D_MODEL = 1024
DEPTH = 4
N_A = DEPTH // 2
N_B = DEPTH - N_A
GLA_HEADS = 4
GLA_DK = D_MODEL // 2 // GLA_HEADS
GLA_DV = D_MODEL // GLA_HEADS
GLA_GATE_RANK = 16
GLA_TAU = 16.0
MOBA_HEADS = 8
HEAD_DIM = D_MODEL // MOBA_HEADS
MOBA_BLOCK = 256
MOBA_TOPK = 3
ROPE_THETA = 10000.0
PEER_HEADS = 8
PEER_NKEYS = 128
PEER_EXPERTS = PEER_NKEYS * PEER_NKEYS
PEER_DKEY = 256
PEER_TOPK = 16
DN_ALPHA = (2.0 * DEPTH) ** 0.25
LN_EPS = 1e-5

LANES = 128
VMEM_LIMIT = 56 * 1024 * 1024
NEG_INF = float("-inf")


def _cparams(*sem):
    return pltpu.CompilerParams(dimension_semantics=sem, vmem_limit_bytes=VMEM_LIMIT)


def _row_tile(n_rows, want):
    t = min(want, n_rows)
    while n_rows % t:
        t //= 2
    return t


def _mod_operand(m, seq_len, tm):
    d = m.shape[-1]
    if seq_len % tm == 0:
        per = seq_len // tm
        return m[:, None, :], (1, 1, d), (lambda i: (i // per, 0, 0))
    rows = jnp.repeat(m, seq_len, axis=0)
    return rows.reshape(rows.shape[0] // tm, tm, d), (1, tm, d), (lambda i: (i, 0, 0))


def _layernorm_rows(z, g, b):
    mu = jnp.mean(z, axis=-1, keepdims=True)
    zc = z - mu
    var = jnp.mean(zc * zc, axis=-1, keepdims=True)
    return zc * lax.rsqrt(var + LN_EPS) * g + b


def _ada_kernel(c_ref, w_ref, b_ref, o_ref):
    c = c_ref[...]
    s = c * jax.nn.sigmoid(c)
    o_ref[0] = jnp.dot(s.astype(BF16), w_ref[0].astype(BF16), preferred_element_type=F32) + b_ref[0]


def _ada_all(c, w, b):
    g, d, n = w.shape
    r = c.shape[0]
    tn = 1024
    return pl.pallas_call(
        _ada_kernel,
        grid=(g, n // tn),
        in_specs=[pl.BlockSpec((r, d), lambda gi, j: (0, 0)),
                  pl.BlockSpec((1, d, tn), lambda gi, j: (gi, 0, j)),
                  pl.BlockSpec((1, 1, tn), lambda gi, j: (gi, 0, j))],
        out_specs=pl.BlockSpec((1, r, tn), lambda gi, j: (gi, 0, j)),
        out_shape=jax.ShapeDtypeStruct((g, r, n), F32),
        compiler_params=_cparams("parallel", "parallel"),
        name="ada",
    )(c, w, b)


def _rope_cols(y, cos, sin, heads):
    outs = []
    for hh in range(heads):
        yh = y[:, hh * HEAD_DIM:(hh + 1) * HEAD_DIM]
        outs.append(yh * cos + pltpu.roll(yh, HEAD_DIM // 2, axis=1) * sin)
    return outs


def _proj_kernel(x_ref, sc_ref, sh_ref, w_ref, *rest, rope_heads, out_cols, out_scale):
    if rope_heads:
        cos_ref, sin_ref = rest[:2]
        outs = rest[2:]
    else:
        outs = rest
    h = x_ref[...] * (1.0 + sc_ref[0]) + sh_ref[0]
    y = jnp.dot(h.astype(BF16), w_ref[...], preferred_element_type=F32)
    if out_scale != 1.0:
        y = y * out_scale
    pieces = []
    if rope_heads:
        parts = _rope_cols(y, cos_ref[...], sin_ref[...], rope_heads)
        pieces += [(hh * HEAD_DIM, p) for hh, p in enumerate(parts)]
    plain = rope_heads * HEAD_DIM
    for lo, hi in sorted({(max(lo, plain), hi) for _, lo, hi in out_cols if hi > plain}):
        pieces.append((lo, y[:, lo:hi]))
    for o, (_, lo, hi) in zip(outs, out_cols):
        for start, vals in pieces:
            if lo <= start and start + vals.shape[1] <= hi:
                o[:, start - lo:start - lo + vals.shape[1]] = vals.astype(o.dtype)


def _proj(x, scale, shift, w_bf16, seq_len, out_cols, rope=None, rope_heads=0, out_scale=1.0, tm_want=512):
    n, d = x.shape
    n_out = w_bf16.shape[1]
    tm = _row_tile(n, tm_want)
    sc, sc_blk, sc_idx = _mod_operand(scale, seq_len, tm)
    sh, sh_blk, sh_idx = _mod_operand(shift, seq_len, tm)
    in_specs = [pl.BlockSpec((tm, d), lambda i: (i, 0)),
                pl.BlockSpec(sc_blk, sc_idx),
                pl.BlockSpec(sh_blk, sh_idx),
                pl.BlockSpec((d, n_out), lambda i: (0, 0))]
    args = [x, sc, sh, w_bf16]
    if rope_heads:
        cos, sin = rope
        per = cos.shape[0] // tm
        in_specs += [pl.BlockSpec((tm, HEAD_DIM), lambda i: (i % per, 0)),
                     pl.BlockSpec((tm, HEAD_DIM), lambda i: (i % per, 0))]
        args += [cos, sin]
    outs = pl.pallas_call(
        functools.partial(_proj_kernel, rope_heads=rope_heads, out_cols=tuple(out_cols), out_scale=out_scale),
        grid=(n // tm,),
        in_specs=in_specs,
        out_specs=[pl.BlockSpec((tm, hi - lo), lambda i: (i, 0)) for _, lo, hi in out_cols],
        out_shape=[jax.ShapeDtypeStruct((n, hi - lo), dt) for dt, lo, hi in out_cols],
        compiler_params=_cparams("parallel"),
        name="proj",
    )(*args)
    return outs


def _out_ln_kernel(a_ref, w_ref, x_ref, gate_ref, g_ref, b_ref, o_ref):
    y = jnp.dot(a_ref[...], w_ref[...], preferred_element_type=F32)
    z = DN_ALPHA * x_ref[...] + (1.0 + gate_ref[0]) * y
    o_ref[...] = _layernorm_rows(z, g_ref[...], b_ref[...])


def _out_ln(a_bf16, w_bf16, x, gate, ln_g, ln_b, seq_len, tm_want=512):
    n, d = x.shape
    k = a_bf16.shape[1]
    tm = _row_tile(n, tm_want)
    gt, gt_blk, gt_idx = _mod_operand(gate, seq_len, tm)
    return pl.pallas_call(
        _out_ln_kernel,
        grid=(n // tm,),
        in_specs=[pl.BlockSpec((tm, k), lambda i: (i, 0)),
                  pl.BlockSpec((k, d), lambda i: (0, 0)),
                  pl.BlockSpec((tm, d), lambda i: (i, 0)),
                  pl.BlockSpec(gt_blk, gt_idx),
                  pl.BlockSpec((1, d), lambda i: (0, 0)),
                  pl.BlockSpec((1, d), lambda i: (0, 0))],
        out_specs=pl.BlockSpec((tm, d), lambda i: (i, 0)),
        out_shape=jax.ShapeDtypeStruct((n, d), F32),
        compiler_params=_cparams("parallel"),
        name="out_ln",
    )(a_bf16, w_bf16, x, gt, ln_g.reshape(1, d), ln_b.reshape(1, d))


def _log_sigmoid(z):
    return jnp.minimum(z, 0.0) - jnp.log(1.0 + jnp.exp(-jnp.abs(z)))


def _gla_kernel(q_ref, k_ref, v_ref, r_ref, g1_ref, wg2_ref, bg_ref, ng_ref, s0_ref,
                og_ref, sfin_ref, s_scr, *, chunk, sub, seq_len, padded_len):
    c = pl.program_id(1)
    n_chunks = padded_len // chunk

    @pl.when(c == 0)
    def _():
        s_scr[...] = s0_ref[0]

    z = jnp.dot(g1_ref[0], wg2_ref[...], precision=lax.Precision.HIGHEST,
                preferred_element_type=F32) + bg_ref[...]
    la = _log_sigmoid(z) * (1.0 / GLA_TAU)
    live = None
    if padded_len != seq_len:
        row = c * chunk + lax.broadcasted_iota(jnp.int32, (chunk, 1), 0)
        live = row < seq_len
        la = jnp.where(live, la, 0.0)
    n_live = min(chunk, seq_len) if n_chunks == 1 else chunk

    ri = lax.broadcasted_iota(jnp.int32, (chunk, chunk), 0)
    ci = lax.broadcasted_iota(jnp.int32, (chunk, chunk), 1)
    tri = (ri >= ci).astype(F32)
    bc_all = jnp.dot(tri, la, precision=lax.Precision.HIGHEST, preferred_element_type=F32)

    for h in range(GLA_HEADS):
        kcols = slice(h * GLA_DK, (h + 1) * GLA_DK)
        vcols = slice(h * GLA_DV, (h + 1) * GLA_DV)
        _gla_head(q_ref[0, :, kcols] * (GLA_DK ** -0.5), k_ref[0, :, kcols], v_ref[0, :, vcols],
                  r_ref[0, :, vcols], bc_all[:, kcols], live, ng_ref, og_ref, vcols, s_scr, h,
                  chunk=chunk, sub=sub, n_live=n_live)

    @pl.when(c == n_chunks - 1)
    def _():
        sfin_ref[0] = s_scr[...]


def _gla_head(q, k, v, r, bc, live, ng_ref, og_ref, vcols, s_scr, h, *, chunk, sub, n_live):
    if live is not None:
        k = jnp.where(live, k, 0.0)
        v = jnp.where(live, v, 0.0)
    s = s_scr[h]
    o = jnp.dot((q * jnp.exp(bc)).astype(BF16), s.astype(BF16), preferred_element_type=F32)
    sub_row = lax.broadcasted_iota(jnp.int32, (sub, 1), 0)
    o_parts = []
    for blk in range(chunk // sub):
        lo = blk * sub
        b_i = bc[lo:lo + sub]
        q_i = q[lo:lo + sub]
        o_i = o[lo:lo + sub]
        if blk > 0:
            b_ref = bc[lo - 1:lo]
            q_t = q_i * jnp.exp(b_i - b_ref)
            k_t = k[:lo] * jnp.exp(b_ref - bc[:lo])
            att = lax.dot_general(q_t.astype(BF16), k_t.astype(BF16), (((1,), (1,)), ((), ())),
                                  preferred_element_type=F32)
            o_i = o_i + jnp.dot(att.astype(BF16), v[:lo].astype(BF16), preferred_element_type=F32)
        for j in range(min(sub, max(n_live - lo, 0))):
            b_j = bc[lo + j:lo + j + 1]
            e = jnp.exp(jnp.where(sub_row >= j, b_i - b_j, NEG_INF))
            w_col = jnp.sum(q_i * k[lo + j:lo + j + 1] * e, axis=-1, keepdims=True)
            o_i = o_i + w_col * v[lo + j:lo + j + 1]
        o_parts.append(o_i)
    o = jnp.concatenate(o_parts, axis=0) if len(o_parts) > 1 else o_parts[0]

    o = o * lax.rsqrt(jnp.mean(o * o, axis=-1, keepdims=True) + LN_EPS) * ng_ref[...]
    og_ref[0, :, vcols] = (o * (r * jax.nn.sigmoid(r))).astype(og_ref.dtype)

    b_last = bc[chunk - 1:chunk]
    k_d = k * jnp.exp(b_last - bc)
    decay = jnp.transpose(jnp.broadcast_to(jnp.exp(b_last), (GLA_DK, GLA_DK)))
    kv = lax.dot_general(k_d.astype(BF16), v.astype(BF16), (((0,), (0,)), ((), ())),
                         preferred_element_type=F32)
    s_scr[h] = jnp.concatenate([decay] * (GLA_DV // GLA_DK), axis=1) * s + kv


def _gla_scan(proj, s0, wg2_pad, b_g, norm_g, batch, seq_len):
    cols = proj.shape[1]
    if seq_len >= 64:
        chunk, sub = 64, 16
    else:
        chunk = sub = 16
    padded = -(-seq_len // chunk) * chunk
    p3 = proj.reshape(batch, seq_len, cols)
    if padded != seq_len:
        p3 = jnp.pad(p3, ((0, 0), (0, padded - seq_len), (0, 0)))
    n_chunks = padded // chunk
    dq = GLA_HEADS * GLA_DK
    dv = GLA_HEADS * GLA_DV
    k0, v0, r0, g0 = 1, 2 * dq // dv, (2 * dq + dv) // dv, (2 * dq + 2 * dv) // LANES
    state_spec = pl.BlockSpec((1, GLA_HEADS, GLA_DK, GLA_DV), lambda b, c: (b, 0, 0, 0))
    og, s_fin = pl.pallas_call(
        functools.partial(_gla_kernel, chunk=chunk, sub=sub, seq_len=seq_len, padded_len=padded),
        grid=(batch, n_chunks),
        in_specs=[pl.BlockSpec((1, chunk, dq), lambda b, c: (b, c, 0)),
                  pl.BlockSpec((1, chunk, dq), lambda b, c: (b, c, k0)),
                  pl.BlockSpec((1, chunk, dv), lambda b, c: (b, c, v0)),
                  pl.BlockSpec((1, chunk, dv), lambda b, c: (b, c, r0)),
                  pl.BlockSpec((1, chunk, LANES), lambda b, c: (b, c, g0)),
                  pl.BlockSpec((LANES, dq), lambda b, c: (0, 0)),
                  pl.BlockSpec((1, dq), lambda b, c: (0, 0)),
                  pl.BlockSpec((1, GLA_DV), lambda b, c: (0, 0)),
                  state_spec],
        out_specs=[pl.BlockSpec((1, chunk, dv), lambda b, c: (b, c, 0)), state_spec],
        out_shape=[jax.ShapeDtypeStruct((batch, padded, dv), BF16),
                   jax.ShapeDtypeStruct((batch, GLA_HEADS, GLA_DK, GLA_DV), F32)],
        scratch_shapes=[pltpu.VMEM((GLA_HEADS, GLA_DK, GLA_DV), F32)],
        compiler_params=_cparams("parallel", "arbitrary"),
        name="gla_scan",
    )(p3, p3, p3, p3, p3, wg2_pad, b_g.reshape(1, dq), norm_g.reshape(1, GLA_DV), s0)
    og = og[:, :seq_len].reshape(batch * seq_len, GLA_HEADS * GLA_DV)
    return og, s_fin


def _gla_layer(x, mods, s0, w_in_pad, wg2_pad, b_g, norm_g, w_o, ln_g, ln_b, batch, seq_len):
    shift, scale, gate = mods
    (proj,) = _proj(x, scale, shift, w_in_pad, seq_len, [(F32, 0, w_in_pad.shape[1])])
    og, s_fin = _gla_scan(proj, s0, wg2_pad, b_g, norm_g, batch, seq_len)
    return _out_ln(og, w_o, x, gate, ln_g, ln_b, seq_len), s_fin


def _top_values(s, count, want_rank=False):
    vals = []
    rank = jnp.full(s.shape, float(count), F32) if want_rank else None
    for r in range(count):
        m = jnp.max(s, axis=0, keepdims=True)
        vals.append(m)
        hit = s == m
        if want_rank:
            rank = jnp.where(hit, float(r), rank)
        if r + 1 < count:
            s = jnp.where(hit, NEG_INF, s)
    return (vals, rank) if want_rank else vals


def _peer_select(q_scr, keys_ref, c_scr, e1_scr, r2_scr, e2_scr, head):
    row0 = pl.multiple_of(head * PEER_DKEY, PEER_DKEY)
    half = PEER_DKEY // 2
    s1_all = jnp.dot(keys_ref[0], q_scr[pl.ds(row0, half), :],
                     precision=lax.Precision.HIGHEST, preferred_element_type=F32)
    s2_all = jnp.dot(keys_ref[1], q_scr[pl.ds(row0 + half, half), :],
                     precision=lax.Precision.HIGHEST, preferred_element_type=F32)
    width = min(LANES, s1_all.shape[1])
    for g in range(s1_all.shape[1] // width):
        cols = slice(g * width, (g + 1) * width)
        _peer_select_group(s1_all[:, cols], s2_all[:, cols], c_scr, e1_scr, r2_scr, e2_scr, head, cols)


def _peer_select_group(s1, s2, c_scr, e1_scr, r2_scr, e2_scr, head, cols):
    a = _top_values(s1, PEER_TOPK)
    b, rank2 = _top_values(s2, PEER_TOPK, want_rank=True)
    s1 = s1 - a[0]
    s2 = s2 - b[0]
    a = [v - a[0] for v in a]
    b = [v - b[0] for v in b]
    cand = [a[i] + b[j] for i in range(PEER_TOPK) for j in range(PEER_TOPK) if (i + 1) * (j + 1) <= PEER_TOPK]
    pad = (-len(cand)) % 8
    cand = jnp.concatenate(cand + [jnp.full_like(cand[0], NEG_INF)] * pad, axis=0)
    top = _top_values(cand, PEER_TOPK)
    cut = top[-1]
    z = jnp.zeros_like(cut)
    for v in top:
        z = z + jnp.exp(v)
    count = jnp.zeros_like(s1)
    for bj in b:
        count = count + jnp.where(s1 + bj >= cut, 1.0, 0.0)
    c_scr[head, :, cols] = count
    e1_scr[head, :, cols] = jnp.exp(s1) / z
    r2_scr[head, :, cols] = rank2.astype(BF16)
    e2_scr[head, :, cols] = jnp.exp(s2).astype(BF16)


def _peer_kernel(x_ref, sc_ref, sh_ref, gate_ref, wq_ref, keys_ref, u_ref, vt_ref, lng_ref, lnb_ref,
                 o_ref, ht_scr, q_scr, c_scr, e1_scr, r2_scr, e2_scr, acc_scr, a_scr, p_scr, *, te, eb):
    j = pl.program_id(1)

    @pl.when(j == 0)
    def _():
        h = x_ref[...] * (1.0 + sc_ref[0]) + sh_ref[0]
        ht_scr[...] = jnp.transpose(h).astype(BF16)
        q_scr[...] = jnp.dot(wq_ref[...], ht_scr[...], preferred_element_type=F32)

        def head_body(head, carry):
            _peer_select(q_scr, keys_ref, c_scr, e1_scr, r2_scr, e2_scr, head)
            return carry
        lax.fori_loop(0, PEER_HEADS, head_body, 0)
        acc_scr[...] = jnp.zeros_like(acc_scr)

    def activations(e):
        a_scr[e % 2] = jnp.dot(u_ref[e * eb:(e + 1) * eb, :], ht_scr[...],
                               preferred_element_type=F32)

    def weigh(e):
        key1 = (j * te + e * eb) // PEER_NKEYS
        for i1 in range(eb // PEER_NKEYS):
            rows = slice(i1 * PEER_NKEYS, (i1 + 1) * PEER_NKEYS)
            w = None
            for head in range(PEER_HEADS):
                row = pl.ds(key1 + i1, 1)
                keep = r2_scr[head] < c_scr[head, row, :].astype(BF16)
                val = jnp.where(keep, e2_scr[head] * e1_scr[head, row, :].astype(BF16), jnp.zeros((), BF16))
                w = val if w is None else w + val
            a_t = a_scr[e % 2, rows, :]
            gelu = 0.5 * a_t * (1.0 + lax.erf(a_t * (0.5 ** 0.5)))
            p_scr[e % 2, rows, :] = w * gelu.astype(BF16)

    def combine(e):
        acc_scr[...] += jnp.dot(vt_ref[:, e * eb:(e + 1) * eb], p_scr[e % 2],
                                preferred_element_type=F32)

    n_blocks = te // eb
    activations(0)
    for e in range(n_blocks):
        if e + 1 < n_blocks:
            activations(e + 1)
        weigh(e)
        combine(e)

    @pl.when(j == pl.num_programs(1) - 1)
    def _():
        out = jnp.transpose(acc_scr[...])
        z = DN_ALPHA * x_ref[...] + (1.0 + gate_ref[0]) * out
        o_ref[...] = _layernorm_rows(z, lng_ref[...], lnb_ref[...])


def _peer_layer(x, mods, wq_t, keys, u_bf16, v_t, ln_g, ln_b, seq_len, tm_want=512, te=2048, eb=512):
    shift, scale, gate = mods
    n, d = x.shape
    tm = _row_tile(n, tm_want)
    n_exp = u_bf16.shape[0]
    sc, sc_blk, sc_idx = _mod_operand(scale, seq_len, tm)
    sh, sh_blk, sh_idx = _mod_operand(shift, seq_len, tm)
    gt, gt_blk, gt_idx = _mod_operand(gate, seq_len, tm)
    hq = PEER_HEADS * PEER_DKEY
    row = lambda f: (lambda i, j: f(i))
    return pl.pallas_call(
        functools.partial(_peer_kernel, te=te, eb=eb),
        grid=(n // tm, n_exp // te),
        in_specs=[pl.BlockSpec((tm, d), lambda i, j: (i, 0)),
                  pl.BlockSpec(sc_blk, row(sc_idx)),
                  pl.BlockSpec(sh_blk, row(sh_idx)),
                  pl.BlockSpec(gt_blk, row(gt_idx)),
                  pl.BlockSpec((hq, d), lambda i, j: (0, 0)),
                  pl.BlockSpec((2, PEER_NKEYS, PEER_DKEY // 2), lambda i, j: (0, 0, 0)),
                  pl.BlockSpec((te, d), lambda i, j: (j, 0)),
                  pl.BlockSpec((d, te), lambda i, j: (0, j)),
                  pl.BlockSpec((1, d), lambda i, j: (0, 0)),
                  pl.BlockSpec((1, d), lambda i, j: (0, 0))],
        out_specs=pl.BlockSpec((tm, d), lambda i, j: (i, 0)),
        out_shape=jax.ShapeDtypeStruct((n, d), F32),
        scratch_shapes=[pltpu.VMEM((d, tm), BF16),
                        pltpu.VMEM((hq, tm), F32),
                        pltpu.VMEM((PEER_HEADS, PEER_NKEYS, tm), F32),
                        pltpu.VMEM((PEER_HEADS, PEER_NKEYS, tm), F32),
                        pltpu.VMEM((PEER_HEADS, PEER_NKEYS, tm), BF16),
                        pltpu.VMEM((PEER_HEADS, PEER_NKEYS, tm), BF16),
                        pltpu.VMEM((d, tm), F32),
                        pltpu.VMEM((2, eb, tm), F32),
                        pltpu.VMEM((2, eb, tm), BF16)],
        compiler_params=_cparams("parallel", "arbitrary"),
        name="peer",
    )(x, sc, sh, gt, wq_t, keys, u_bf16, v_t, ln_g.reshape(1, d), ln_b.reshape(1, d))


def _block_mean_kernel(*refs, n_parts):
    o_ref = refs[-1]
    total = None
    rows = 0
    for p_ref in refs[-1 - n_parts:-1]:
        part = jnp.sum(p_ref[0].astype(F32), axis=0, keepdims=True)
        total = part if total is None else total + part
        rows += p_ref.shape[1]
    o_ref[0] = total * (1.0 / rows)


def _block_means_dense(k, batch, seq_len):
    w = k.shape[1]
    nb = seq_len // MOBA_BLOCK
    out = pl.pallas_call(
        functools.partial(_block_mean_kernel, n_parts=1),
        grid=(batch * nb,),
        in_specs=[pl.BlockSpec((1, MOBA_BLOCK, w), lambda i: (i, 0, 0))],
        out_specs=pl.BlockSpec((1, 1, w), lambda i: (i, 0, 0)),
        out_shape=jax.ShapeDtypeStruct((batch * nb, 1, w), F32),
        compiler_params=_cparams("parallel"),
        name="block_means",
    )(k.reshape(batch * nb, MOBA_BLOCK, w))
    return out.reshape(batch, nb, w)


def _moba_prompt_kernel(qi_ref, kv_ref, q_ref, k_ref, v_ref, mean_ref, o_ref,
                        m_scr, acc_scr, qa_scr):
    t = pl.program_id(1)
    qi = qi_ref[t]
    kvi = kv_ref[t]
    own = kvi == qi
    blk = MOBA_BLOCK
    masked = 1e30
    row = lax.broadcasted_iota(jnp.int32, (blk, blk), 0)
    col = lax.broadcasted_iota(jnp.int32, (blk, blk), 1)
    causal_bias = jnp.where(own, jnp.where(col <= row, 0.0, NEG_INF), 0.0)
    gcol_i = lax.broadcasted_iota(jnp.int32, (blk, LANES), 1)
    head_cols = [slice(h * HEAD_DIM, (h + 1) * HEAD_DIM) for h in range(MOBA_HEADS)]

    @pl.when(own)
    def _():
        gcol = gcol_i.astype(F32)
        n_past = qi.astype(F32)
        for h, cs in enumerate(head_cols):
            g = lax.dot_general(q_ref[0, :, cs].astype(F32), mean_ref[0, :, cs], (((1,), (1,)), ((), ())),
                                precision=lax.Precision.HIGHEST, preferred_element_type=F32)
            g = jnp.where(gcol < n_past, g, NEG_INF)
            bias = jnp.where(gcol == n_past, 0.0, -masked)
            for r in range(MOBA_TOPK):
                best = jnp.max(g, axis=-1, keepdims=True)
                pick = jnp.min(jnp.where(g == best, gcol, float(LANES)), axis=-1, keepdims=True)
                hit = gcol == pick
                bias = jnp.where(r < qi, jnp.where(hit, 0.0, bias), bias)
                g = jnp.where(hit, NEG_INF, g)
            qa_scr[h] = jnp.concatenate([q_ref[0, :, cs], bias.astype(BF16)], axis=1)
            m_scr[h] = jnp.full((blk, LANES), NEG_INF, F32)
            acc_scr[h] = jnp.zeros((blk, 2 * HEAD_DIM), F32)

    onehot = jnp.where(gcol_i == kvi, 1.0, 0.0).astype(BF16)
    ones = jnp.ones((blk, HEAD_DIM), BF16)
    twice = lambda a: jnp.concatenate([a, a], axis=1)
    for h, cs in enumerate(head_cols):
        k_aug = jnp.concatenate([k_ref[0, :, cs], onehot], axis=1)
        s = lax.dot_general(qa_scr[h], k_aug, (((1,), (1,)), ((), ())), preferred_element_type=F32) + causal_bias
        m_old = m_scr[h]
        m_new = jnp.maximum(m_old, jnp.max(s, axis=-1, keepdims=True))
        alpha = jnp.exp(m_old - m_new)
        p = jnp.exp(s - twice(m_new))
        v_aug = jnp.concatenate([v_ref[0, :, cs], ones], axis=1)
        acc_scr[h] = twice(alpha) * acc_scr[h] + jnp.dot(p.astype(BF16), v_aug, preferred_element_type=F32)
        m_scr[h] = m_new

    @pl.when(kvi == 0)
    def _():
        for h, cs in enumerate(head_cols):
            acc = acc_scr[h]
            o_ref[0, :, cs] = (acc[:, :HEAD_DIM] / acc[:, HEAD_DIM:]).astype(o_ref.dtype)


def _moba_prompt(q, k, v, means, batch, seq_len):
    w = q.shape[1]
    nb = seq_len // MOBA_BLOCK
    assert nb <= LANES
    means = jnp.pad(means, ((0, 0), (0, LANES - nb), (0, 0)))
    qi_tab = jnp.asarray([qi for qi in range(nb) for _ in range(qi + 1)], jnp.int32)
    kv_tab = jnp.asarray([qi - s for qi in range(nb) for s in range(qi + 1)], jnp.int32)
    r3 = lambda a: a.reshape(batch, seq_len, w)
    out = pl.pallas_call(
        _moba_prompt_kernel,
        grid_spec=pltpu.PrefetchScalarGridSpec(
            num_scalar_prefetch=2,
            grid=(batch, int(qi_tab.shape[0])),
            in_specs=[pl.BlockSpec((1, MOBA_BLOCK, w), lambda b, t, qt, kt: (b, qt[t], 0)),
                      pl.BlockSpec((1, MOBA_BLOCK, w), lambda b, t, qt, kt: (b, kt[t], 0)),
                      pl.BlockSpec((1, MOBA_BLOCK, w), lambda b, t, qt, kt: (b, kt[t], 0)),
                      pl.BlockSpec((1, LANES, w), lambda b, t, qt, kt: (b, 0, 0))],
            out_specs=pl.BlockSpec((1, MOBA_BLOCK, w), lambda b, t, qt, kt: (b, qt[t], 0)),
            scratch_shapes=[pltpu.VMEM((MOBA_HEADS, MOBA_BLOCK, LANES), F32),
                            pltpu.VMEM((MOBA_HEADS, MOBA_BLOCK, 2 * HEAD_DIM), F32),
                            pltpu.VMEM((MOBA_HEADS, MOBA_BLOCK, HEAD_DIM + LANES), BF16)]),
        out_shape=jax.ShapeDtypeStruct((batch, seq_len, w), BF16),
        compiler_params=_cparams("parallel", "arbitrary"),
        name="moba_prompt",
    )(qi_tab, kv_tab, r3(q), r3(k), r3(v), means)
    return out.reshape(batch * seq_len, w)


def _moba_decode_kernel(pt_ref, q_ref, kn_ref, vn_ref, *rest, n_new, n_past, ppb):
    k_pages = rest[:ppb]
    v_pages = rest[ppb:2 * ppb]
    o_ref, g_scr, m_scr, l_scr, acc_scr = rest[2 * ppb:]
    j = pl.program_id(1)
    nq = MOBA_HEADS * n_new
    head_bits = MOBA_HEADS.bit_length() - 1
    new_bits = n_new.bit_length() - 1
    q = q_ref[0]
    qb = q.astype(BF16)

    def same_head(n_cols):
        lane_head = lax.broadcasted_iota(jnp.int32, (nq, n_cols), 1) & (MOBA_HEADS - 1)
        row_head = lax.broadcasted_iota(jnp.int32, (nq, n_cols), 0) >> new_bits
        return lane_head == row_head

    def scores(keys):
        return lax.dot_general(qb, keys, (((1,), (1,)), ((), ())), preferred_element_type=F32)

    page_len = k_pages[0].shape[1]
    page_rows = page_len * MOBA_HEADS
    pages_per_block = MOBA_BLOCK // page_len
    head_bias = jnp.where(same_head(page_rows), 0.0, NEG_INF)
    for bi in range(ppb // pages_per_block):
        block = j * (ppb // pages_per_block) + bi
        k_blk = k_pages[bi * pages_per_block:(bi + 1) * pages_per_block]
        v_blk = v_pages[bi * pages_per_block:(bi + 1) * pages_per_block]
        key_sum = None
        for k_ref in k_blk:
            part = jnp.sum(k_ref[0], axis=0)
            key_sum = part if key_sum is None else key_sum + part
        gate = lax.dot_general(q, key_sum * (1.0 / MOBA_BLOCK), (((1,), (1,)), ((), ())),
                               precision=lax.Precision.HIGHEST, preferred_element_type=F32)
        g_scr[block] = jnp.sum(jnp.where(same_head(MOBA_HEADS), gate, 0.0), axis=-1, keepdims=True)
        s_pages = [scores(k_ref[0].reshape(page_rows, HEAD_DIM).astype(BF16)) + head_bias for k_ref in k_blk]
        m = None
        for s in s_pages:
            s_max = jnp.max(s, axis=-1, keepdims=True)
            m = s_max if m is None else jnp.maximum(m, s_max)
        l = None
        acc = None
        for s, v_ref in zip(s_pages, v_blk):
            p = jnp.exp(s - m)
            p_sum = jnp.sum(p, axis=-1, keepdims=True)
            pv = jnp.dot(p.astype(BF16), v_ref[0].reshape(page_rows, HEAD_DIM).astype(BF16),
                         preferred_element_type=F32)
            l = p_sum if l is None else l + p_sum
            acc = pv if acc is None else acc + pv
        m_scr[block] = m
        l_scr[block] = l
        acc_scr[block] = acc

    @pl.when(j == pl.num_programs(1) - 1)
    def _():
        s = scores(kn_ref[0].astype(BF16))
        key_t = lax.broadcasted_iota(jnp.int32, s.shape, 1) >> head_bits
        qry_r = lax.broadcasted_iota(jnp.int32, s.shape, 0) & (n_new - 1)
        s = jnp.where(same_head(s.shape[1]), jnp.where(key_t <= qry_r, s, NEG_INF), NEG_INF)
        m_own = jnp.max(s, axis=-1, keepdims=True)
        p = jnp.exp(s - m_own)
        l_own = jnp.sum(p, axis=-1, keepdims=True)
        acc_own = jnp.dot(p.astype(BF16), vn_ref[0].astype(BF16), preferred_element_type=F32)

        k_sel = min(MOBA_TOPK, n_past)
        gates = [g_scr[b] for b in range(n_past)]
        weights = []
        m_all = m_own
        for b in range(n_past):
            place = jnp.zeros_like(gates[b])
            for o in range(n_past):
                if o != b:
                    ahead = (gates[o] >= gates[b]) if o < b else (gates[o] > gates[b])
                    place = place + jnp.where(ahead, 1.0, 0.0)
            weights.append(jnp.where(place < k_sel, 0.0, NEG_INF))
            m_all = jnp.maximum(m_all, m_scr[b] + weights[b])
        w_own = jnp.exp(m_own - m_all)
        l_all = w_own * l_own
        acc_all = w_own * acc_own
        for b in range(n_past):
            w_b = jnp.exp(m_scr[b] + weights[b] - m_all)
            l_all = l_all + w_b * l_scr[b]
            acc_all = acc_all + w_b * acc_scr[b]
        o_ref[0] = acc_all / l_all


def _moba_decode(q, k_new, v_new, cache_k, cache_v, page_table_flat, batch, n_new, pages_per_seq):
    _, page, heads, hd = cache_k.shape
    assert heads & (heads - 1) == 0 and n_new & (n_new - 1) == 0
    n_past = pages_per_seq * page // MOBA_BLOCK
    blocks_per_step = 2 if n_past % 2 == 0 else 1
    ppb = (MOBA_BLOCK // page) * blocks_per_step
    nq = heads * n_new
    q_hq = q.reshape(batch, n_new, heads, hd).transpose(0, 2, 1, 3).reshape(batch, nq, hd)
    kv_rows = lambda a: a.reshape(batch, nq, hd)
    seq_spec = pl.BlockSpec((1, nq, hd), lambda b, j, pt: (b, 0, 0))
    page_spec = lambda part: pl.BlockSpec(
        (1, page, heads, hd), lambda b, j, pt: (pt[b * pages_per_seq + j * ppb + part], 0, 0, 0))
    out = pl.pallas_call(
        functools.partial(_moba_decode_kernel, n_new=n_new, n_past=n_past, ppb=ppb),
        grid_spec=pltpu.PrefetchScalarGridSpec(
            num_scalar_prefetch=1,
            grid=(batch, n_past // blocks_per_step),
            in_specs=[seq_spec, seq_spec, seq_spec] + [page_spec(p) for p in range(ppb)] * 2,
            out_specs=seq_spec,
            scratch_shapes=[pltpu.VMEM((n_past, nq, 1), F32),
                            pltpu.VMEM((n_past, nq, 1), F32),
                            pltpu.VMEM((n_past, nq, 1), F32),
                            pltpu.VMEM((n_past, nq, hd), F32)]),
        out_shape=jax.ShapeDtypeStruct((batch, nq, hd), F32),
        compiler_params=_cparams("parallel", "arbitrary"),
        name="moba_decode",
    )(page_table_flat, q_hq, kv_rows(k_new), kv_rows(v_new),
      *([cache_k] * ppb), *([cache_v] * ppb))
    return out.reshape(batch, heads, n_new, hd).transpose(0, 2, 1, 3).reshape(batch * n_new, heads * hd)


def _rope_tables(pos):
    half = HEAD_DIM // 2
    inv = ROPE_THETA ** (-jnp.arange(half, dtype=F32) / half)
    ang = pos.astype(F32)[:, None] * inv[None, :]
    cos, sin = jnp.cos(ang), jnp.sin(ang)
    return jnp.concatenate([cos, cos], axis=-1), jnp.concatenate([-sin, sin], axis=-1)


def _trunk(x, mods, mods_kv, s0, past, p, batch, seq_len):
    pos0 = 0 if past is None else past[3] * past[0].shape[1]
    pos = pos0 + jnp.arange(seq_len, dtype=jnp.int32)
    cos, sin = _rope_tables(pos)
    n = batch * seq_len
    if seq_len % _row_tile(n, 512):
        cos, sin = jnp.tile(cos, (batch, 1)), jnp.tile(sin, (batch, 1))
    gla_fin = []
    k_new = v_new = None
    dm = MOBA_HEADS * HEAD_DIM
    for layer in range(DEPTH):
        shift, scale, gate = mods[layer][0]
        if layer < N_A:
            x, s_fin = _gla_layer(x, mods[layer][0], s0[layer], p["gla_w_in"][layer], p["gla_w_g2"][layer],
                                  p["gla_b_g"][layer], p["gla_norm_g"][layer], p["gla_w_o"][layer],
                                  p["ln_g"][layer, 0], p["ln_b"][layer, 0], batch, seq_len)
            gla_fin.append(s_fin)
        else:
            jb = layer - N_A
            (q,) = _proj(x, scale, shift, p["moba_w_q"][jb], seq_len, [(BF16 if past is None else F32, 0, dm)],
                         rope=(cos, sin), rope_heads=MOBA_HEADS, out_scale=HEAD_DIM ** -0.5)
            if past is None:
                o = _moba_prompt(q, k_bf16, v_bf16, means, batch, seq_len)
            else:
                o = _moba_decode(q, k_new, v_new, past[0], past[1], past[2], batch, seq_len,
                                 past[3]).astype(BF16)
            x = _out_ln(o, p["moba_w_o"][jb], x, gate, p["ln_g"][layer, 0], p["ln_b"][layer, 0], seq_len)
        x = _peer_layer(x, mods[layer][1], p["peer_w_q"][layer], p["peer_keys"][layer], p["peer_u"][layer],
                        p["peer_v"][layer], p["ln_g"][layer, 1], p["ln_b"][layer, 1], seq_len)
        if layer == N_A - 1:
            kv_shift, kv_scale = mods_kv
            kv_cols = [(F32, 0, dm), (F32, dm, 2 * dm)]
            if past is None:
                kv_cols += [(BF16, 0, dm), (BF16, dm, 2 * dm)]
            kv = _proj(x, kv_scale, kv_shift, p["w_kv"], seq_len, kv_cols, rope=(cos, sin),
                       rope_heads=MOBA_HEADS)
            k_new, v_new = kv[:2]
            if past is None:
                k_bf16, v_bf16 = kv[2:]
                means = _block_means_dense(k_new, batch, seq_len)
    return x, jnp.stack(gla_fin), k_new, v_new


def kernel(x_prompt, x_sample, c_prompt, c_sample, state_gla, cache_k, cache_v, page_table,
           w_ada, b_ada, ln_g, ln_b, gla_w_in, gla_w_g2, gla_b_g, gla_norm_g, gla_w_o,
           w_ada_kv, b_ada_kv, w_kv, moba_w_q, moba_w_o, peer_w_q, peer_keys, peer_u, peer_v):
    b_p, l_p, d = x_prompt.shape
    b_s, l_s, _ = x_sample.shape
    gla_cols = -(-gla_w_in.shape[-1] // LANES) * LANES
    p = {
        "ln_g": ln_g, "ln_b": ln_b,
        "gla_w_in": jnp.pad(gla_w_in, ((0, 0), (0, 0), (0, gla_cols - gla_w_in.shape[-1]))).astype(BF16),
        "gla_w_g2": jnp.pad(gla_w_g2, ((0, 0), (0, LANES - GLA_GATE_RANK), (0, 0))),
        "gla_b_g": gla_b_g, "gla_norm_g": gla_norm_g,
        "gla_w_o": gla_w_o.astype(BF16),
        "w_kv": w_kv.astype(BF16),
        "moba_w_q": moba_w_q.astype(BF16), "moba_w_o": moba_w_o.astype(BF16),
        "peer_w_q": jnp.swapaxes(peer_w_q, 1, 2).astype(BF16),
        "peer_keys": peer_keys,
        "peer_u": peer_u.astype(BF16),
        "peer_v": jnp.swapaxes(peer_v, 1, 2).astype(BF16),
    }
    c_all = jnp.concatenate([c_prompt, c_sample], axis=0)
    rows = c_all.shape[0]
    c_all = jnp.pad(c_all, ((0, (-rows) % 8), (0, 0)))
    m_all = _ada_all(c_all, w_ada.reshape(DEPTH * 2, d, 3 * d), b_ada.reshape(DEPTH * 2, 1, 3 * d))
    m_kv = _ada_all(c_all, w_ada_kv[None], b_ada_kv[None, None])[0]

    def group_mods(lo, hi):
        mods = [[tuple(m_all[layer * 2 + sub, lo:hi, i * d:(i + 1) * d] for i in range(3))
                 for sub in range(2)] for layer in range(DEPTH)]
        return mods, (m_kv[lo:hi, :d], m_kv[lo:hi, d:])

    mods_p, kv_p = group_mods(0, b_p)
    mods_s, kv_s = group_mods(b_p, b_p + b_s)

    zero_state = jnp.zeros((N_A, b_p) + state_gla.shape[2:], F32)
    y_p, gla_p, k_p, v_p = _trunk(x_prompt.reshape(b_p * l_p, d), mods_p, kv_p, zero_state, None, p, b_p, l_p)

    past = (cache_k, cache_v, page_table.reshape(-1).astype(jnp.int32), page_table.shape[1])
    y_s, gla_s, k_s, v_s = _trunk(x_sample.reshape(b_s * l_s, d), mods_s, kv_s, state_gla, past, p, b_s, l_s)

    heads = lambda a, b, l: a.reshape(b, l, MOBA_HEADS, HEAD_DIM)
    return (y_p.reshape(b_p, l_p, d), y_s.reshape(b_s, l_s, d), gla_p, gla_s,
            heads(k_p, b_p, l_p), heads(v_p, b_p, l_p), heads(k_s, b_s, l_s), heads(v_s, b_s, l_s))
```

```python
import functools
import math

import jax
import jax.numpy as jnp
from jax import lax
from jax.experimental import pallas as pl
from jax.experimental.pallas import tpu as pltpu

F32 = jnp.float32
BF16 = jnp.bfloat16

D_MODEL = 1024
DEPTH = 4
N_A = DEPTH // 2
N_B = DEPTH - N_A
GLA_HEADS = 4
GLA_DK = D_MODEL // 2 // GLA_HEADS
GLA_DV = D_MODEL // GLA_HEADS
GLA_GATE_RANK = 16
GLA_TAU = 16.0
MOBA_HEADS = 8
HEAD_DIM = D_MODEL // MOBA_HEADS
MOBA_BLOCK = 256
MOBA_TOPK = 3
ROPE_THETA = 10000.0
PEER_HEADS = 8
PEER_NKEYS = 128
PEER_EXPERTS = PEER_NKEYS * PEER_NKEYS
PEER_DKEY = 256
PEER_TOPK = 16
DN_ALPHA = (2.0 * DEPTH) ** 0.25
LN_EPS = 1e-5

LANES = 128
VMEM_LIMIT = 56 * 1024 * 1024
NEG_INF = float("-inf")


def _cparams(*sem):
    return pltpu.CompilerParams(dimension_semantics=sem, vmem_limit_bytes=VMEM_LIMIT)


def _row_tile(n_rows, want):
    t = min(want, n_rows)
    while n_rows % t:
        t //= 2
    return t


def _mod_operand(m, seq_len, tm):
    d = m.shape[-1]
    if seq_len % tm == 0:
        per = seq_len // tm
        return m[:, None, :], (1, 1, d), (lambda i: (i // per, 0, 0))
    rows = jnp.repeat(m, seq_len, axis=0)
    return rows.reshape(rows.shape[0] // tm, tm, d), (1, tm, d), (lambda i: (i, 0, 0))


def _layernorm_rows(z, g, b):
    mu = jnp.mean(z, axis=-1, keepdims=True)
    zc = z - mu
    var = jnp.mean(zc * zc, axis=-1, keepdims=True)
    return zc * lax.rsqrt(var + LN_EPS) * g + b


def _ada_kernel(c_ref, w_ref, b_ref, o_ref):
    c = c_ref[...]
    s = c * jax.nn.sigmoid(c)
    o_ref[0] = jnp.dot(s.astype(BF16), w_ref[0].astype(BF16), preferred_element_type=F32) + b_ref[0]


def _ada_all(c, w, b):
    g, d, n = w.shape
    r = c.shape[0]
    tn = 1024
    return pl.pallas_call(
        _ada_kernel,
        grid=(g, n // tn),
        in_specs=[pl.BlockSpec((r, d), lambda gi, j: (0, 0)),
                  pl.BlockSpec((1, d, tn), lambda gi, j: (gi, 0, j)),
                  pl.BlockSpec((1, 1, tn), lambda gi, j: (gi, 0, j))],
        out_specs=pl.BlockSpec((1, r, tn), lambda gi, j: (gi, 0, j)),
        out_shape=jax.ShapeDtypeStruct((g, r, n), F32),
        compiler_params=_cparams("parallel", "parallel"),
        name="ada",
    )(c, w, b)


def _rope_cols(y, cos, sin, heads):
    outs = []
    for hh in range(heads):
        yh = y[:, hh * HEAD_DIM:(hh + 1) * HEAD_DIM]
        outs.append(yh * cos + pltpu.roll(yh, HEAD_DIM // 2, axis=1) * sin)
    return outs


def _proj_kernel(x_ref, sc_ref, sh_ref, w_ref, *rest, rope_heads, out_cols, out_scale):
    if rope_heads:
        cos_ref, sin_ref = rest[:2]
        outs = rest[2:]
    else:
        outs = rest
    h = x_ref[...] * (1.0 + sc_ref[0]) + sh_ref[0]
    y = jnp.dot(h.astype(BF16), w_ref[...], preferred_element_type=F32)
    if out_scale != 1.0:
        y = y * out_scale
    pieces = []
    if rope_heads:
        parts = _rope_cols(y, cos_ref[...], sin_ref[...], rope_heads)
        pieces += [(hh * HEAD_DIM, p) for hh, p in enumerate(parts)]
    plain = rope_heads * HEAD_DIM
    for lo, hi in sorted({(max(lo, plain), hi) for _, lo, hi in out_cols if hi > plain}):
        pieces.append((lo, y[:, lo:hi]))
    for o, (_, lo, hi) in zip(outs, out_cols):
        for start, vals in pieces:
            if lo <= start and start + vals.shape[1] <= hi:
                o[:, start - lo:start - lo + vals.shape[1]] = vals.astype(o.dtype)


def _proj(x, scale, shift, w_bf16, seq_len, out_cols, rope=None, rope_heads=0, out_scale=1.0, tm_want=512):
    n, d = x.shape
    n_out = w_bf16.shape[1]
    tm = _row_tile(n, tm_want)
    sc, sc_blk, sc_idx = _mod_operand(scale, seq_len, tm)
    sh, sh_blk, sh_idx = _mod_operand(shift, seq_len, tm)
    in_specs = [pl.BlockSpec((tm, d), lambda i: (i, 0)),
                pl.BlockSpec(sc_blk, sc_idx),
                pl.BlockSpec(sh_blk, sh_idx),
                pl.BlockSpec((d, n_out), lambda i: (0, 0))]
    args = [x, sc, sh, w_bf16]
    if rope_heads:
        cos, sin = rope
        per = cos.shape[0] // tm
        in_specs += [pl.BlockSpec((tm, HEAD_DIM), lambda i: (i % per, 0)),
                     pl.BlockSpec((tm, HEAD_DIM), lambda i: (i % per, 0))]
        args += [cos, sin]
    outs = pl.pallas_call(
        functools.partial(_proj_kernel, rope_heads=rope_heads, out_cols=tuple(out_cols), out_scale=out_scale),
        grid=(n // tm,),
        in_specs=in_specs,
        out_specs=[pl.BlockSpec((tm, hi - lo), lambda i: (i, 0)) for _, lo, hi in out_cols],
        out_shape=[jax.ShapeDtypeStruct((n, hi - lo), dt) for dt, lo, hi in out_cols],
        compiler_params=_cparams("parallel"),
        name="proj",
    )(*args)
    return outs


def _out_ln_kernel(a_ref, w_ref, x_ref, gate_ref, g_ref, b_ref, o_ref):
    y = jnp.dot(a_ref[...], w_ref[...], preferred_element_type=F32)
    z = DN_ALPHA * x_ref[...] + (1.0 + gate_ref[0]) * y
    o_ref[...] = _layernorm_rows(z, g_ref[...], b_ref[...])


def _out_ln(a_bf16, w_bf16, x, gate, ln_g, ln_b, seq_len, tm_want=512):
    n, d = x.shape
    k = a_bf16.shape[1]
    tm = _row_tile(n, tm_want)
    gt, gt_blk, gt_idx = _mod_operand(gate, seq_len, tm)
    return pl.pallas_call(
        _out_ln_kernel,
        grid=(n // tm,),
        in_specs=[pl.BlockSpec((tm, k), lambda i: (i, 0)),
                  pl.BlockSpec((k, d), lambda i: (0, 0)),
                  pl.BlockSpec((tm, d), lambda i: (i, 0)),
                  pl.BlockSpec(gt_blk, gt_idx),
                  pl.BlockSpec((1, d), lambda i: (0, 0)),
                  pl.BlockSpec((1, d), lambda i: (0, 0))],
        out_specs=pl.BlockSpec((tm, d), lambda i: (i, 0)),
        out_shape=jax.ShapeDtypeStruct((n, d), F32),
        compiler_params=_cparams("parallel"),
        name="out_ln",
    )(a_bf16, w_bf16, x, gt, ln_g.reshape(1, d), ln_b.reshape(1, d))


def _log_sigmoid(z):
    return jnp.minimum(z, 0.0) - jnp.log(1.0 + jnp.exp(-jnp.abs(z)))


def _gla_kernel(q_ref, k_ref, v_ref, r_ref, g1_ref, wg2_ref, bg_ref, ng_ref, s0_ref,
                og_ref, sfin_ref, s_scr, *, chunk, sub, seq_len, padded_len):
    c = pl.program_id(1)
    n_chunks = padded_len // chunk

    @pl.when(c == 0)
    def _():
        s_scr[...] = s0_ref[0]

    z = jnp.dot(g1_ref[0], wg2_ref[...], precision=lax.Precision.HIGHEST,
                preferred_element_type=F32) + bg_ref[...]
    la = _log_sigmoid(z) * (1.0 / GLA_TAU)
    live = None
    if padded_len != seq_len:
        row = c * chunk + lax.broadcasted_iota(jnp.int32, (chunk, 1), 0)
        live = row < seq_len
        la = jnp.where(live, la, 0.0)
    n_live = min(chunk, seq_len) if n_chunks == 1 else chunk

    ri = lax.broadcasted_iota(jnp.int32, (chunk, chunk), 0)
    ci = lax.broadcasted_iota(jnp.int32, (chunk, chunk), 1)
    tri = (ri >= ci).astype(F32)
    bc_all = jnp.dot(tri, la, precision=lax.Precision.HIGHEST, preferred_element_type=F32)

    for h in range(GLA_HEADS):
        kcols = slice(h * GLA_DK, (h + 1) * GLA_DK)
        vcols = slice(h * GLA_DV, (h + 1) * GLA_DV)
        _gla_head(q_ref[0, :, kcols] * (GLA_DK ** -0.5), k_ref[0, :, kcols], v_ref[0, :, vcols],
                  r_ref[0, :, vcols], bc_all[:, kcols], live, ng_ref, og_ref, vcols, s_scr, h,
                  chunk=chunk, sub=sub, n_live=n_live)

    @pl.when(c == n_chunks - 1)
    def _():
        sfin_ref[0] = s_scr[...]


def _gla_head(q, k, v, r, bc, live, ng_ref, og_ref, vcols, s_scr, h, *, chunk, sub, n_live):
    if live is not None:
        k = jnp.where(live, k, 0.0)
        v = jnp.where(live, v, 0.0)
    s = s_scr[h]
    o = jnp.dot((q * jnp.exp(bc)).astype(BF16), s.astype(BF16), preferred_element_type=F32)
    sub_row = lax.broadcasted_iota(jnp.int32, (sub, 1), 0)
    o_parts = []
    for blk in range(chunk // sub):
        lo = blk * sub
        b_i = bc[lo:lo + sub]
        q_i = q[lo:lo + sub]
        o_i = o[lo:lo + sub]
        if blk > 0:
            b_ref = bc[lo - 1:lo]
            q_t = q_i * jnp.exp(b_i - b_ref)
            k_t = k[:lo] * jnp.exp(b_ref - bc[:lo])
            att = lax.dot_general(q_t.astype(BF16), k_t.astype(BF16), (((1,), (1,)), ((), ())),
                                  preferred_element_type=F32)
            o_i = o_i + jnp.dot(att.astype(BF16), v[:lo].astype(BF16), preferred_element_type=F32)
        for j in range(min(sub, max(n_live - lo, 0))):
            b_j = bc[lo + j:lo + j + 1]
            e = jnp.exp(jnp.where(sub_row >= j, b_i - b_j, NEG_INF))
            w_col = jnp.sum(q_i * k[lo + j:lo + j + 1] * e, axis=-1, keepdims=True)
            o_i = o_i + w_col * v[lo + j:lo + j + 1]
        o_parts.append(o_i)
    o = jnp.concatenate(o_parts, axis=0) if len(o_parts) > 1 else o_parts[0]

    o = o * lax.rsqrt(jnp.mean(o * o, axis=-1, keepdims=True) + LN_EPS) * ng_ref[...]
    og_ref[0, :, vcols] = (o * (r * jax.nn.sigmoid(r))).astype(og_ref.dtype)

    b_last = bc[chunk - 1:chunk]
    k_d = k * jnp.exp(b_last - bc)
    decay = jnp.transpose(jnp.broadcast_to(jnp.exp(b_last), (GLA_DK, GLA_DK)))
    kv = lax.dot_general(k_d.astype(BF16), v.astype(BF16), (((0,), (0,)), ((), ())),
                         preferred_element_type=F32)
    s_scr[h] = jnp.concatenate([decay] * (GLA_DV // GLA_DK), axis=1) * s + kv


def _gla_scan(proj, s0, wg2_pad, b_g, norm_g, batch, seq_len):
    cols = proj.shape[1]
    if seq_len >= 64:
        chunk, sub = 64, 16
    else:
        chunk = sub = 16
    padded = -(-seq_len // chunk) * chunk
    p3 = proj.reshape(batch, seq_len, cols)
    if padded != seq_len:
        p3 = jnp.pad(p3, ((0, 0), (0, padded - seq_len), (0, 0)))
    n_chunks = padded // chunk
    dq = GLA_HEADS * GLA_DK
    dv = GLA_HEADS * GLA_DV
    k0, v0, r0, g0 = 1, 2 * dq // dv, (2 * dq + dv) // dv, (2 * dq + 2 * dv) // LANES
    state_spec = pl.BlockSpec((1, GLA_HEADS, GLA_DK, GLA_DV), lambda b, c: (b, 0, 0, 0))
    og, s_fin = pl.pallas_call(
        functools.partial(_gla_kernel, chunk=chunk, sub=sub, seq_len=seq_len, padded_len=padded),
        grid=(batch, n_chunks),
        in_specs=[pl.BlockSpec((1, chunk, dq), lambda b, c: (b, c, 0)),
                  pl.BlockSpec((1, chunk, dq), lambda b, c: (b, c, k0)),
                  pl.BlockSpec((1, chunk, dv), lambda b, c: (b, c, v0)),
                  pl.BlockSpec((1, chunk, dv), lambda b, c: (b, c, r0)),
                  pl.BlockSpec((1, chunk, LANES), lambda b, c: (b, c, g0)),
                  pl.BlockSpec((LANES, dq), lambda b, c: (0, 0)),
                  pl.BlockSpec((1, dq), lambda b, c: (0, 0)),
                  pl.BlockSpec((1, GLA_DV), lambda b, c: (0, 0)),
                  state_spec],
        out_specs=[pl.BlockSpec((1, chunk, dv), lambda b, c: (b, c, 0)), state_spec],
        out_shape=[jax.ShapeDtypeStruct((batch, padded, dv), BF16),
                   jax.ShapeDtypeStruct((batch, GLA_HEADS, GLA_DK, GLA_DV), F32)],
        scratch_shapes=[pltpu.VMEM((GLA_HEADS, GLA_DK, GLA_DV), F32)],
        compiler_params=_cparams("parallel", "arbitrary"),
        name="gla_scan",
    )(p3, p3, p3, p3, p3, wg2_pad, b_g.reshape(1, dq), norm_g.reshape(1, GLA_DV), s0)
    og = og[:, :seq_len].reshape(batch * seq_len, GLA_HEADS * GLA_DV)
    return og, s_fin


def _gla_layer(x, mods, s0, w_in_pad, wg2_pad, b_g, norm_g, w_o, ln_g, ln_b, batch, seq_len):
    shift, scale, gate = mods
    (proj,) = _proj(x, scale, shift, w_in_pad, seq_len, [(F32, 0, w_in_pad.shape[1])])
    og, s_fin = _gla_scan(proj, s0, wg2_pad, b_g, norm_g, batch, seq_len)
    return _out_ln(og, w_o, x, gate, ln_g, ln_b, seq_len), s_fin


def _top_values(s, count, want_rank=False):
    vals = []
    rank = jnp.full(s.shape, float(count), F32) if want_rank else None
    for r in range(count):
        m = jnp.max(s, axis=0, keepdims=True)
        vals.append(m)
        hit = s == m
        if want_rank:
            rank = jnp.where(hit, float(r), rank)
        if r + 1 < count:
            s = jnp.where(hit, NEG_INF, s)
    return (vals, rank) if want_rank else vals


def _peer_select(q_scr, keys_ref, c_scr, e1_scr, r2_scr, e2_scr, head):
    row0 = pl.multiple_of(head * PEER_DKEY, PEER_DKEY)
    half = PEER_DKEY // 2
    s1_all = jnp.dot(keys_ref[0], q_scr[pl.ds(row0, half), :],
                     precision=lax.Precision.HIGHEST, preferred_element_type=F32)
    s2_all = jnp.dot(keys_ref[1], q_scr[pl.ds(row0 + half, half), :],
                     precision=lax.Precision.HIGHEST, preferred_element_type=F32)
    width = min(LANES, s1_all.shape[1])
    for g in range(s1_all.shape[1] // width):
        cols = slice(g * width, (g + 1) * width)
        _peer_select_group(s1_all[:, cols], s2_all[:, cols], c_scr, e1_scr, r2_scr, e2_scr, head, cols)


def _peer_select_group(s1, s2, c_scr, e1_scr, r2_scr, e2_scr, head, cols):
    a = _top_values(s1, PEER_TOPK)
    b, rank2 = _top_values(s2, PEER_TOPK, want_rank=True)
    s1 = s1 - a[0]
    s2 = s2 - b[0]
    a = [v - a[0] for v in a]
    b = [v - b[0] for v in b]
    cand = [a[i] + b[j] for i in range(PEER_TOPK) for j in range(PEER_TOPK) if (i + 1) * (j + 1) <= PEER_TOPK]
    pad = (-len(cand)) % 8
    cand = jnp.concatenate(cand + [jnp.full_like(cand[0], NEG_INF)] * pad, axis=0)
    top = _top_values(cand, PEER_TOPK)
    cut = top[-1]
    z = jnp.zeros_like(cut)
    for v in top:
        z = z + jnp.exp(v)
    count = jnp.zeros_like(s1)
    for bj in b:
        count = count + jnp.where(s1 + bj >= cut, 1.0, 0.0)
    c_scr[head, :, cols] = count
    e1_scr[head, :, cols] = jnp.exp(s1) / z
    r2_scr[head, :, cols] = rank2.astype(BF16)
    e2_scr[head, :, cols] = jnp.exp(s2).astype(BF16)


def _peer_kernel(x_ref, sc_ref, sh_ref, gate_ref, wq_ref, keys_ref, u_ref, vt_ref, lng_ref, lnb_ref,
                 o_ref, ht_scr, q_scr, c_scr, e1_scr, r2_scr, e2_scr, acc_scr, a_scr, p_scr, *, te, eb):
    j = pl.program_id(1)

    @pl.when(j == 0)
    def _():
        h = x_ref[...] * (1.0 + sc_ref[0]) + sh_ref[0]
        ht_scr[...] = jnp.transpose(h).astype(BF16)
        q_scr[...] = jnp.dot(wq_ref[...], ht_scr[...], preferred_element_type=F32)

        def head_body(head, carry):
            _peer_select(q_scr, keys_ref, c_scr, e1_scr, r2_scr, e2_scr, head)
            return carry
        lax.fori_loop(0, PEER_HEADS, head_body, 0, unroll=2)
        acc_scr[...] = jnp.zeros_like(acc_scr)

    def activations(e):
        a_scr[e % 2] = jnp.dot(u_ref[e * eb:(e + 1) * eb, :], ht_scr[...],
                               preferred_element_type=F32)

    def weigh(e):
        key1 = (j * te + e * eb) // PEER_NKEYS
        for i1 in range(eb // PEER_NKEYS):
            rows = slice(i1 * PEER_NKEYS, (i1 + 1) * PEER_NKEYS)
            w = None
            for head in range(PEER_HEADS):
                row = pl.ds(key1 + i1, 1)
                keep = r2_scr[head] < c_scr[head, row, :].astype(BF16)
                val = jnp.where(keep, e2_scr[head] * e1_scr[head, row, :].astype(BF16), jnp.zeros((), BF16))
                w = val if w is None else w + val
            a_half = a_scr[e % 2, rows, :]
            gelu = a_half * (1.0 + lax.erf(a_half * (2.0 ** 0.5)))
            p_scr[e % 2, rows, :] = w * gelu.astype(BF16)

    def combine(e):
        acc_scr[...] += jnp.dot(vt_ref[:, e * eb:(e + 1) * eb], p_scr[e % 2],
                                preferred_element_type=F32)

    n_blocks = te // eb
    activations(0)
    for e in range(n_blocks):
        if e + 1 < n_blocks:
            activations(e + 1)
        weigh(e)
        combine(e)

    @pl.when(j == pl.num_programs(1) - 1)
    def _():
        out = jnp.transpose(acc_scr[...])
        z = DN_ALPHA * x_ref[...] + (1.0 + gate_ref[0]) * out
        o_ref[...] = _layernorm_rows(z, lng_ref[...], lnb_ref[...])


def _peer_layer(x, mods, wq_t, keys, u_half, v_t, ln_g, ln_b, seq_len, tm_want=512, te=2048, eb=512):
    shift, scale, gate = mods
    n, d = x.shape
    tm = _row_tile(n, tm_want)
    n_exp = u_half.shape[0]
    sc, sc_blk, sc_idx = _mod_operand(scale, seq_len, tm)
    sh, sh_blk, sh_idx = _mod_operand(shift, seq_len, tm)
    gt, gt_blk, gt_idx = _mod_operand(gate, seq_len, tm)
    hq = PEER_HEADS * PEER_DKEY
    row = lambda f: (lambda i, j: f(i))
    return pl.pallas_call(
        functools.partial(_peer_kernel, te=te, eb=eb),
        grid=(n // tm, n_exp // te),
        in_specs=[pl.BlockSpec((tm, d), lambda i, j: (i, 0)),
                  pl.BlockSpec(sc_blk, row(sc_idx)),
                  pl.BlockSpec(sh_blk, row(sh_idx)),
                  pl.BlockSpec(gt_blk, row(gt_idx)),
                  pl.BlockSpec((hq, d), lambda i, j: (0, 0)),
                  pl.BlockSpec((2, PEER_NKEYS, PEER_DKEY // 2), lambda i, j: (0, 0, 0)),
                  pl.BlockSpec((te, d), lambda i, j: (j, 0)),
                  pl.BlockSpec((d, te), lambda i, j: (0, j)),
                  pl.BlockSpec((1, d), lambda i, j: (0, 0)),
                  pl.BlockSpec((1, d), lambda i, j: (0, 0))],
        out_specs=pl.BlockSpec((tm, d), lambda i, j: (i, 0)),
        out_shape=jax.ShapeDtypeStruct((n, d), F32),
        scratch_shapes=[pltpu.VMEM((d, tm), BF16),
                        pltpu.VMEM((hq, tm), F32),
                        pltpu.VMEM((PEER_HEADS, PEER_NKEYS, tm), F32),
                        pltpu.VMEM((PEER_HEADS, PEER_NKEYS, tm), F32),
                        pltpu.VMEM((PEER_HEADS, PEER_NKEYS, tm), BF16),
                        pltpu.VMEM((PEER_HEADS, PEER_NKEYS, tm), BF16),
                        pltpu.VMEM((d, tm), F32),
                        pltpu.VMEM((2, eb, tm), F32),
                        pltpu.VMEM((2, eb, tm), BF16)],
        compiler_params=_cparams("parallel", "arbitrary"),
        name="peer",
    )(x, sc, sh, gt, wq_t, keys, u_half, v_t, ln_g.reshape(1, d), ln_b.reshape(1, d))


def _block_mean_kernel(*refs, n_parts):
    o_ref = refs[-1]
    total = None
    rows = 0
    for p_ref in refs[-1 - n_parts:-1]:
        part = jnp.sum(p_ref[0].astype(F32), axis=0, keepdims=True)
        total = part if total is None else total + part
        rows += p_ref.shape[1]
    o_ref[0] = total * (1.0 / rows)


def _block_means_dense(k, batch, seq_len):
    w = k.shape[1]
    nb = seq_len // MOBA_BLOCK
    out = pl.pallas_call(
        functools.partial(_block_mean_kernel, n_parts=1),
        grid=(batch * nb,),
        in_specs=[pl.BlockSpec((1, MOBA_BLOCK, w), lambda i: (i, 0, 0))],
        out_specs=pl.BlockSpec((1, 1, w), lambda i: (i, 0, 0)),
        out_shape=jax.ShapeDtypeStruct((batch * nb, 1, w), F32),
        compiler_params=_cparams("parallel"),
        name="block_means",
    )(k.reshape(batch * nb, MOBA_BLOCK, w))
    return out.reshape(batch, nb, w)


def _moba_prompt_kernel(qi_ref, kv_ref, q_ref, k_ref, v_ref, mean_ref, o_ref,
                        m_scr, acc_scr, qa_scr):
    t = pl.program_id(1)
    qi = qi_ref[t]
    kvi = kv_ref[t]
    own = kvi == qi
    blk = MOBA_BLOCK
    masked = 1e30
    row = lax.broadcasted_iota(jnp.int32, (blk, blk), 0)
    col = lax.broadcasted_iota(jnp.int32, (blk, blk), 1)
    causal_bias = jnp.where(own, jnp.where(col <= row, 0.0, NEG_INF), 0.0)
    gcol_i = lax.broadcasted_iota(jnp.int32, (blk, LANES), 1)
    head_cols = [slice(h * HEAD_DIM, (h + 1) * HEAD_DIM) for h in range(MOBA_HEADS)]

    @pl.when(own)
    def _():
        gcol = gcol_i.astype(F32)
        n_past = qi.astype(F32)
        for h, cs in enumerate(head_cols):
            g = lax.dot_general(q_ref[0, :, cs].astype(F32), mean_ref[0, :, cs], (((1,), (1,)), ((), ())),
                                precision=lax.Precision.HIGHEST, preferred_element_type=F32)
            g = jnp.where(gcol < n_past, g, NEG_INF)
            bias = jnp.where(gcol == n_past, 0.0, -masked)
            for r in range(MOBA_TOPK):
                best = jnp.max(g, axis=-1, keepdims=True)
                pick = jnp.min(jnp.where(g == best, gcol, float(LANES)), axis=-1, keepdims=True)
                hit = gcol == pick
                bias = jnp.where(r < qi, jnp.where(hit, 0.0, bias), bias)
                g = jnp.where(hit, NEG_INF, g)
            qa_scr[h] = jnp.concatenate([q_ref[0, :, cs], bias.astype(BF16)], axis=1)
            m_scr[h] = jnp.full((blk, LANES), NEG_INF, F32)
            acc_scr[h] = jnp.zeros((blk, 2 * HEAD_DIM), F32)

    onehot = jnp.where(gcol_i == kvi, 1.0, 0.0).astype(BF16)
    ones = jnp.ones((blk, HEAD_DIM), BF16)
    twice = lambda a: jnp.concatenate([a, a], axis=1)
    for h, cs in enumerate(head_cols):
        k_aug = jnp.concatenate([k_ref[0, :, cs], onehot], axis=1)
        s = lax.dot_general(qa_scr[h], k_aug, (((1,), (1,)), ((), ())), preferred_element_type=F32) + causal_bias
        m_old = m_scr[h]
        m_new = jnp.maximum(m_old, jnp.max(s, axis=-1, keepdims=True))
        alpha = jnp.exp(m_old - m_new)
        p = jnp.exp(s - twice(m_new))
        v_aug = jnp.concatenate([v_ref[0, :, cs], ones], axis=1)
        acc_scr[h] = twice(alpha) * acc_scr[h] + jnp.dot(p.astype(BF16), v_aug, preferred_element_type=F32)
        m_scr[h] = m_new

    @pl.when(kvi == 0)
    def _():
        for h, cs in enumerate(head_cols):
            acc = acc_scr[h]
            o_ref[0, :, cs] = (acc[:, :HEAD_DIM] / acc[:, HEAD_DIM:]).astype(o_ref.dtype)


def _moba_prompt(q, k, v, means, batch, seq_len):
    w = q.shape[1]
    nb = seq_len // MOBA_BLOCK
    assert nb <= LANES
    means = jnp.pad(means, ((0, 0), (0, LANES - nb), (0, 0)))
    qi_tab = jnp.asarray([qi for qi in range(nb) for _ in range(qi + 1)], jnp.int32)
    kv_tab = jnp.asarray([qi - s for qi in range(nb) for s in range(qi + 1)], jnp.int32)
    r3 = lambda a: a.reshape(batch, seq_len, w)
    out = pl.pallas_call(
        _moba_prompt_kernel,
        grid_spec=pltpu.PrefetchScalarGridSpec(
            num_scalar_prefetch=2,
            grid=(batch, int(qi_tab.shape[0])),
            in_specs=[pl.BlockSpec((1, MOBA_BLOCK, w), lambda b, t, qt, kt: (b, qt[t], 0)),
                      pl.BlockSpec((1, MOBA_BLOCK, w), lambda b, t, qt, kt: (b, kt[t], 0)),
                      pl.BlockSpec((1, MOBA_BLOCK, w), lambda b, t, qt, kt: (b, kt[t], 0)),
                      pl.BlockSpec((1, LANES, w), lambda b, t, qt, kt: (b, 0, 0))],
            out_specs=pl.BlockSpec((1, MOBA_BLOCK, w), lambda b, t, qt, kt: (b, qt[t], 0)),
            scratch_shapes=[pltpu.VMEM((MOBA_HEADS, MOBA_BLOCK, LANES), F32),
                            pltpu.VMEM((MOBA_HEADS, MOBA_BLOCK, 2 * HEAD_DIM), F32),
                            pltpu.VMEM((MOBA_HEADS, MOBA_BLOCK, HEAD_DIM + LANES), BF16)]),
        out_shape=jax.ShapeDtypeStruct((batch, seq_len, w), BF16),
        compiler_params=_cparams("parallel", "arbitrary"),
        name="moba_prompt",
    )(qi_tab, kv_tab, r3(q), r3(k), r3(v), means)
    return out.reshape(batch * seq_len, w)


def _moba_decode_kernel(pt_ref, q_ref, kn_ref, vn_ref, *rest, n_new, n_past, ppb):
    k_pages = rest[:ppb]
    v_pages = rest[ppb:2 * ppb]
    o_ref, g_scr, m_scr, l_scr, acc_scr = rest[2 * ppb:]
    j = pl.program_id(1)
    nq = MOBA_HEADS * n_new
    head_bits = MOBA_HEADS.bit_length() - 1
    new_bits = n_new.bit_length() - 1
    q = q_ref[0]
    qb = q.astype(BF16)

    def same_head(n_cols):
        lane_head = lax.broadcasted_iota(jnp.int32, (nq, n_cols), 1) & (MOBA_HEADS - 1)
        row_head = lax.broadcasted_iota(jnp.int32, (nq, n_cols), 0) >> new_bits
        return lane_head == row_head

    def scores(keys):
        return lax.dot_general(qb, keys, (((1,), (1,)), ((), ())), preferred_element_type=F32)

    page_len = k_pages[0].shape[1]
    page_rows = page_len * MOBA_HEADS
    pages_per_block = MOBA_BLOCK // page_len
    head_bias = jnp.where(same_head(page_rows), 0.0, NEG_INF)
    for bi in range(ppb // pages_per_block):
        block = j * (ppb // pages_per_block) + bi
        k_blk = k_pages[bi * pages_per_block:(bi + 1) * pages_per_block]
        v_blk = v_pages[bi * pages_per_block:(bi + 1) * pages_per_block]
        key_sum = None
        for k_ref in k_blk:
            part = jnp.sum(k_ref[0], axis=0)
            key_sum = part if key_sum is None else key_sum + part
        gate = lax.dot_general(q, key_sum * (1.0 / MOBA_BLOCK), (((1,), (1,)), ((), ())),
                               precision=lax.Precision.HIGHEST, preferred_element_type=F32)
        g_scr[block] = jnp.sum(jnp.where(same_head(MOBA_HEADS), gate, 0.0), axis=-1, keepdims=True)
        s_pages = [scores(k_ref[0].reshape(page_rows, HEAD_DIM).astype(BF16)) + head_bias for k_ref in k_blk]
        m = None
        for s in s_pages:
            s_max = jnp.max(s, axis=-1, keepdims=True)
            m = s_max if m is None else jnp.maximum(m, s_max)
        l = None
        acc = None
        for s, v_ref in zip(s_pages, v_blk):
            p = jnp.exp(s - m)
            p_sum = jnp.sum(p, axis=-1, keepdims=True)
            pv = jnp.dot(p.astype(BF16), v_ref[0].reshape(page_rows, HEAD_DIM).astype(BF16),
                         preferred_element_type=F32)
            l = p_sum if l is None else l + p_sum
            acc = pv if acc is None else acc + pv
        m_scr[block] = m
        l_scr[block] = l
        acc_scr[block] = acc

    @pl.when(j == pl.num_programs(1) - 1)
    def _():
        s = scores(kn_ref[0].astype(BF16))
        key_t = lax.broadcasted_iota(jnp.int32, s.shape, 1) >> head_bits
        qry_r = lax.broadcasted_iota(jnp.int32, s.shape, 0) & (n_new - 1)
        s = jnp.where(same_head(s.shape[1]), jnp.where(key_t <= qry_r, s, NEG_INF), NEG_INF)
        m_own = jnp.max(s, axis=-1, keepdims=True)
        p = jnp.exp(s - m_own)
        l_own = jnp.sum(p, axis=-1, keepdims=True)
        acc_own = jnp.dot(p.astype(BF16), vn_ref[0].astype(BF16), preferred_element_type=F32)

        k_sel = min(MOBA_TOPK, n_past)
        gates = [g_scr[b] for b in range(n_past)]
        weights = []
        m_all = m_own
        for b in range(n_past):
            place = jnp.zeros_like(gates[b])
            for o in range(n_past):
                if o != b:
                    ahead = (gates[o] >= gates[b]) if o < b else (gates[o] > gates[b])
                    place = place + jnp.where(ahead, 1.0, 0.0)
            weights.append(jnp.where(place < k_sel, 0.0, NEG_INF))
            m_all = jnp.maximum(m_all, m_scr[b] + weights[b])
        w_own = jnp.exp(m_own - m_all)
        l_all = w_own * l_own
        acc_all = w_own * acc_own
        for b in range(n_past):
            w_b = jnp.exp(m_scr[b] + weights[b] - m_all)
            l_all = l_all + w_b * l_scr[b]
            acc_all = acc_all + w_b * acc_scr[b]
        o_ref[0] = acc_all / l_all


def _moba_decode(q, k_new, v_new, cache_k, cache_v, page_table_flat, batch, n_new, pages_per_seq):
    _, page, heads, hd = cache_k.shape
    assert heads & (heads - 1) == 0 and n_new & (n_new - 1) == 0
    n_past = pages_per_seq * page // MOBA_BLOCK
    blocks_per_step = next(c for c in (4, 2, 1) if n_past % c == 0)
    ppb = (MOBA_BLOCK // page) * blocks_per_step
    nq = heads * n_new
    q_hq = q.reshape(batch, n_new, heads, hd).transpose(0, 2, 1, 3).reshape(batch, nq, hd)
    kv_rows = lambda a: a.reshape(batch, nq, hd)
    seq_spec = pl.BlockSpec((1, nq, hd), lambda b, j, pt: (b, 0, 0))
    page_spec = lambda part: pl.BlockSpec(
        (1, page, heads, hd), lambda b, j, pt: (pt[b * pages_per_seq + j * ppb + part], 0, 0, 0))
    out = pl.pallas_call(
        functools.partial(_moba_decode_kernel, n_new=n_new, n_past=n_past, ppb=ppb),
        grid_spec=pltpu.PrefetchScalarGridSpec(
            num_scalar_prefetch=1,
            grid=(batch, n_past // blocks_per_step),
            in_specs=[seq_spec, seq_spec, seq_spec] + [page_spec(p) for p in range(ppb)] * 2,
            out_specs=seq_spec,
            scratch_shapes=[pltpu.VMEM((n_past, nq, 1), F32),
                            pltpu.VMEM((n_past, nq, 1), F32),
                            pltpu.VMEM((n_past, nq, 1), F32),
                            pltpu.VMEM((n_past, nq, hd), F32)]),
        out_shape=jax.ShapeDtypeStruct((batch, nq, hd), F32),
        compiler_params=_cparams("parallel", "arbitrary"),
        name="moba_decode",
    )(page_table_flat, q_hq, kv_rows(k_new), kv_rows(v_new),
      *([cache_k] * ppb), *([cache_v] * ppb))
    return out.reshape(batch, heads, n_new, hd).transpose(0, 2, 1, 3).reshape(batch * n_new, heads * hd)


def _rope_tables(pos):
    half = HEAD_DIM // 2
    inv = ROPE_THETA ** (-jnp.arange(half, dtype=F32) / half)
    ang = pos.astype(F32)[:, None] * inv[None, :]
    cos, sin = jnp.cos(ang), jnp.sin(ang)
    return jnp.concatenate([cos, cos], axis=-1), jnp.concatenate([-sin, sin], axis=-1)


def _trunk(x, mods, mods_kv, s0, past, p, batch, seq_len):
    pos0 = 0 if past is None else past[3] * past[0].shape[1]
    pos = pos0 + jnp.arange(seq_len, dtype=jnp.int32)
    cos, sin = _rope_tables(pos)
    n = batch * seq_len
    if seq_len % _row_tile(n, 512):
        cos, sin = jnp.tile(cos, (batch, 1)), jnp.tile(sin, (batch, 1))
    gla_fin = []
    k_new = v_new = None
    dm = MOBA_HEADS * HEAD_DIM
    for layer in range(DEPTH):
        shift, scale, gate = mods[layer][0]
        if layer < N_A:
            x, s_fin = _gla_layer(x, mods[layer][0], s0[layer], p["gla_w_in"][layer], p["gla_w_g2"][layer],
                                  p["gla_b_g"][layer], p["gla_norm_g"][layer], p["gla_w_o"][layer],
                                  p["ln_g"][layer, 0], p["ln_b"][layer, 0], batch, seq_len)
            gla_fin.append(s_fin)
        else:
            jb = layer - N_A
            (q,) = _proj(x, scale, shift, p["moba_w_q"][jb], seq_len, [(BF16 if past is None else F32, 0, dm)],
                         rope=(cos, sin), rope_heads=MOBA_HEADS, out_scale=HEAD_DIM ** -0.5)
            if past is None:
                o = _moba_prompt(q, k_bf16, v_bf16, means, batch, seq_len)
            else:
                o = _moba_decode(q, k_new, v_new, past[0], past[1], past[2], batch, seq_len,
                                 past[3]).astype(BF16)
            x = _out_ln(o, p["moba_w_o"][jb], x, gate, p["ln_g"][layer, 0], p["ln_b"][layer, 0], seq_len)
        x = _peer_layer(x, mods[layer][1], p["peer_w_q"][layer], p["peer_keys"][layer], p["peer_u"][layer],
                        p["peer_v"][layer], p["ln_g"][layer, 1], p["ln_b"][layer, 1], seq_len)
        if layer == N_A - 1:
            kv_shift, kv_scale = mods_kv
            kv_cols = [(F32, 0, dm), (F32, dm, 2 * dm)]
            if past is None:
                kv_cols += [(BF16, 0, dm), (BF16, dm, 2 * dm)]
            kv = _proj(x, kv_scale, kv_shift, p["w_kv"], seq_len, kv_cols, rope=(cos, sin),
                       rope_heads=MOBA_HEADS)
            k_new, v_new = kv[:2]
            if past is None:
                k_bf16, v_bf16 = kv[2:]
                means = _block_means_dense(k_new, batch, seq_len)
    return x, jnp.stack(gla_fin), k_new, v_new


def kernel(x_prompt, x_sample, c_prompt, c_sample, state_gla, cache_k, cache_v, page_table,
           w_ada, b_ada, ln_g, ln_b, gla_w_in, gla_w_g2, gla_b_g, gla_norm_g, gla_w_o,
           w_ada_kv, b_ada_kv, w_kv, moba_w_q, moba_w_o, peer_w_q, peer_keys, peer_u, peer_v):
    b_p, l_p, d = x_prompt.shape
    b_s, l_s, _ = x_sample.shape
    gla_cols = -(-gla_w_in.shape[-1] // LANES) * LANES
    p = {
        "ln_g": ln_g, "ln_b": ln_b,
        "gla_w_in": jnp.pad(gla_w_in, ((0, 0), (0, 0), (0, gla_cols - gla_w_in.shape[-1]))).astype(BF16),
        "gla_w_g2": jnp.pad(gla_w_g2, ((0, 0), (0, LANES - GLA_GATE_RANK), (0, 0))),
        "gla_b_g": gla_b_g, "gla_norm_g": gla_norm_g,
        "gla_w_o": gla_w_o.astype(BF16),
        "w_kv": w_kv.astype(BF16),
        "moba_w_q": moba_w_q.astype(BF16), "moba_w_o": moba_w_o.astype(BF16),
        "peer_w_q": jnp.swapaxes(peer_w_q, 1, 2).astype(BF16),
        "peer_keys": peer_keys,
        "peer_u": (0.5 * peer_u).astype(BF16),
        "peer_v": jnp.swapaxes(peer_v, 1, 2).astype(BF16),
    }
    c_all = jnp.concatenate([c_prompt, c_sample], axis=0)
    rows = c_all.shape[0]
    c_all = jnp.pad(c_all, ((0, (-rows) % 8), (0, 0)))
    m_all = _ada_all(c_all, w_ada.reshape(DEPTH * 2, d, 3 * d), b_ada.reshape(DEPTH * 2, 1, 3 * d))
    m_kv = _ada_all(c_all, w_ada_kv[None], b_ada_kv[None, None])[0]

    def group_mods(lo, hi):
        mods = [[tuple(m_all[layer * 2 + sub, lo:hi, i * d:(i + 1) * d] for i in range(3))
                 for sub in range(2)] for layer in range(DEPTH)]
        return mods, (m_kv[lo:hi, :d], m_kv[lo:hi, d:])

    mods_p, kv_p = group_mods(0, b_p)
    mods_s, kv_s = group_mods(b_p, b_p + b_s)

    zero_state = jnp.zeros((N_A, b_p) + state_gla.shape[2:], F32)
    y_p, gla_p, k_p, v_p = _trunk(x_prompt.reshape(b_p * l_p, d), mods_p, kv_p, zero_state, None, p, b_p, l_p)

    past = (cache_k, cache_v, page_table.reshape(-1).astype(jnp.int32), page_table.shape[1])
    y_s, gla_s, k_s, v_s = _trunk(x_sample.reshape(b_s * l_s, d), mods_s, kv_s, state_gla, past, p, b_s, l_s)

    heads = lambda a, b, l: a.reshape(b, l, MOBA_HEADS, HEAD_DIM)
    return (y_p.reshape(b_p, l_p, d), y_s.reshape(b_s, l_s, d), gla_p, gla_s,
            heads(k_p, b_p, l_p), heads(v_p, b_p, l_p), heads(k_s, b_s, l_s), heads(v_s, b_s, l_s))
```

```python
import functools
import math

import jax
import jax.numpy as jnp
from jax import lax
from jax.experimental import pallas as pl
from jax.experimental.pallas import tpu as pltpu

F32 = jnp.float32
BF16 = jnp.bfloat16

D_MODEL = 1024
DEPTH = 4
N_A = DEPTH // 2
N_B = DEPTH - N_A
GLA_HEADS = 4
GLA_DK = D_MODEL // 2 // GLA_HEADS
GLA_DV = D_MODEL // GLA_HEADS
GLA_GATE_RANK = 16
GLA_TAU = 16.0
MOBA_HEADS = 8
HEAD_DIM = D_MODEL // MOBA_HEADS
MOBA_BLOCK = 256
MOBA_TOPK = 3
ROPE_THETA = 10000.0
PEER_HEADS = 8
PEER_NKEYS = 128
PEER_EXPERTS = PEER_NKEYS * PEER_NKEYS
PEER_DKEY = 256
PEER_TOPK = 16
DN_ALPHA = (2.0 * DEPTH) ** 0.25
LN_EPS = 1e-5

LANES = 128
VMEM_LIMIT = 56 * 1024 * 1024
NEG_INF = float("-inf")


def _cparams(*sem):
    return pltpu.CompilerParams(dimension_semantics=sem, vmem_limit_bytes=VMEM_LIMIT)


def _row_tile(n_rows, want):
    t = min(want, n_rows)
    while n_rows % t:
        t //= 2
    return t


def _mod_operand(m, seq_len, tm):
    d = m.shape[-1]
    if seq_len % tm == 0:
        per = seq_len // tm
        return m[:, None, :], (1, 1, d), (lambda i: (i // per, 0, 0))
    rows = jnp.repeat(m, seq_len, axis=0)
    return rows.reshape(rows.shape[0] // tm, tm, d), (1, tm, d), (lambda i: (i, 0, 0))


def _layernorm_rows(z, g, b):
    mu = jnp.mean(z, axis=-1, keepdims=True)
    zc = z - mu
    var = jnp.mean(zc * zc, axis=-1, keepdims=True)
    return zc * lax.rsqrt(var + LN_EPS) * g + b


def _ada_kernel(c_ref, w_ref, b_ref, o_ref):
    c = c_ref[...]
    s = c * jax.nn.sigmoid(c)
    o_ref[0] = jnp.dot(s.astype(BF16), w_ref[0].astype(BF16), preferred_element_type=F32) + b_ref[0]


def _ada_all(c, w, b):
    g, d, n = w.shape
    r = c.shape[0]
    tn = 1024
    return pl.pallas_call(
        _ada_kernel,
        grid=(g, n // tn),
        in_specs=[pl.BlockSpec((r, d), lambda gi, j: (0, 0)),
                  pl.BlockSpec((1, d, tn), lambda gi, j: (gi, 0, j)),
                  pl.BlockSpec((1, 1, tn), lambda gi, j: (gi, 0, j))],
        out_specs=pl.BlockSpec((1, r, tn), lambda gi, j: (gi, 0, j)),
        out_shape=jax.ShapeDtypeStruct((g, r, n), F32),
        compiler_params=_cparams("parallel", "parallel"),
        name="ada",
    )(c, w, b)


def _rope_cols(y, cos, sin, heads):
    outs = []
    for hh in range(heads):
        yh = y[:, hh * HEAD_DIM:(hh + 1) * HEAD_DIM]
        outs.append(yh * cos + pltpu.roll(yh, HEAD_DIM // 2, axis=1) * sin)
    return outs


def _proj_kernel(x_ref, sc_ref, sh_ref, w_ref, *rest, rope_heads, out_cols, out_scale):
    if rope_heads:
        cos_ref, sin_ref = rest[:2]
        outs = rest[2:]
    else:
        outs = rest
    h = x_ref[...] * (1.0 + sc_ref[0]) + sh_ref[0]
    y = jnp.dot(h.astype(BF16), w_ref[...], preferred_element_type=F32)
    if out_scale != 1.0:
        y = y * out_scale
    pieces = []
    if rope_heads:
        parts = _rope_cols(y, cos_ref[...], sin_ref[...], rope_heads)
        pieces += [(hh * HEAD_DIM, p) for hh, p in enumerate(parts)]
    plain = rope_heads * HEAD_DIM
    for lo, hi in sorted({(max(lo, plain), hi) for _, lo, hi in out_cols if hi > plain}):
        pieces.append((lo, y[:, lo:hi]))
    for o, (_, lo, hi) in zip(outs, out_cols):
        for start, vals in pieces:
            if lo <= start and start + vals.shape[1] <= hi:
                o[:, start - lo:start - lo + vals.shape[1]] = vals.astype(o.dtype)


def _proj(x, scale, shift, w_bf16, seq_len, out_cols, rope=None, rope_heads=0, out_scale=1.0, tm_want=512):
    n, d = x.shape
    n_out = w_bf16.shape[1]
    tm = _row_tile(n, tm_want)
    sc, sc_blk, sc_idx = _mod_operand(scale, seq_len, tm)
    sh, sh_blk, sh_idx = _mod_operand(shift, seq_len, tm)
    in_specs = [pl.BlockSpec((tm, d), lambda i: (i, 0)),
                pl.BlockSpec(sc_blk, sc_idx),
                pl.BlockSpec(sh_blk, sh_idx),
                pl.BlockSpec((d, n_out), lambda i: (0, 0))]
    args = [x, sc, sh, w_bf16]
    if rope_heads:
        cos, sin = rope
        per = cos.shape[0] // tm
        in_specs += [pl.BlockSpec((tm, HEAD_DIM), lambda i: (i % per, 0)),
                     pl.BlockSpec((tm, HEAD_DIM), lambda i: (i % per, 0))]
        args += [cos, sin]
    outs = pl.pallas_call(
        functools.partial(_proj_kernel, rope_heads=rope_heads, out_cols=tuple(out_cols), out_scale=out_scale),
        grid=(n // tm,),
        in_specs=in_specs,
        out_specs=[pl.BlockSpec((tm, hi - lo), lambda i: (i, 0)) for _, lo, hi in out_cols],
        out_shape=[jax.ShapeDtypeStruct((n, hi - lo), dt) for dt, lo, hi in out_cols],
        compiler_params=_cparams("parallel"),
        name="proj",
    )(*args)
    return outs


def _out_ln_kernel(a_ref, w_ref, x_ref, gate_ref, g_ref, b_ref, o_ref):
    y = jnp.dot(a_ref[...], w_ref[...], preferred_element_type=F32)
    z = DN_ALPHA * x_ref[...] + (1.0 + gate_ref[0]) * y
    o_ref[...] = _layernorm_rows(z, g_ref[...], b_ref[...])


def _out_ln(a_bf16, w_bf16, x, gate, ln_g, ln_b, seq_len, tm_want=512):
    n, d = x.shape
    k = a_bf16.shape[1]
    tm = _row_tile(n, tm_want)
    gt, gt_blk, gt_idx = _mod_operand(gate, seq_len, tm)
    return pl.pallas_call(
        _out_ln_kernel,
        grid=(n // tm,),
        in_specs=[pl.BlockSpec((tm, k), lambda i: (i, 0)),
                  pl.BlockSpec((k, d), lambda i: (0, 0)),
                  pl.BlockSpec((tm, d), lambda i: (i, 0)),
                  pl.BlockSpec(gt_blk, gt_idx),
                  pl.BlockSpec((1, d), lambda i: (0, 0)),
                  pl.BlockSpec((1, d), lambda i: (0, 0))],
        out_specs=pl.BlockSpec((tm, d), lambda i: (i, 0)),
        out_shape=jax.ShapeDtypeStruct((n, d), F32),
        compiler_params=_cparams("parallel"),
        name="out_ln",
    )(a_bf16, w_bf16, x, gt, ln_g.reshape(1, d), ln_b.reshape(1, d))


def _log_sigmoid(z):
    return jnp.minimum(z, 0.0) - jnp.log(1.0 + jnp.exp(-jnp.abs(z)))


def _gla_kernel(q_ref, k_ref, v_ref, r_ref, g1_ref, wg2_ref, bg_ref, ng_ref, s0_ref,
                og_ref, sfin_ref, s_scr, *, chunk, sub, seq_len, padded_len):
    c = pl.program_id(1)
    n_chunks = padded_len // chunk

    @pl.when(c == 0)
    def _():
        s_scr[...] = s0_ref[0]

    z = jnp.dot(g1_ref[0], wg2_ref[...], precision=lax.Precision.HIGHEST,
                preferred_element_type=F32) + bg_ref[...]
    la = _log_sigmoid(z) * (1.0 / GLA_TAU)
    live = None
    if padded_len != seq_len:
        row = c * chunk + lax.broadcasted_iota(jnp.int32, (chunk, 1), 0)
        live = row < seq_len
        la = jnp.where(live, la, 0.0)
    n_live = min(chunk, seq_len) if n_chunks == 1 else chunk

    ri = lax.broadcasted_iota(jnp.int32, (chunk, chunk), 0)
    ci = lax.broadcasted_iota(jnp.int32, (chunk, chunk), 1)
    tri = (ri >= ci).astype(F32)
    bc_all = jnp.dot(tri, la, precision=lax.Precision.HIGHEST, preferred_element_type=F32)

    for h in range(GLA_HEADS):
        kcols = slice(h * GLA_DK, (h + 1) * GLA_DK)
        vcols = slice(h * GLA_DV, (h + 1) * GLA_DV)
        _gla_head(q_ref[0, :, kcols] * (GLA_DK ** -0.5), k_ref[0, :, kcols], v_ref[0, :, vcols],
                  r_ref[0, :, vcols], bc_all[:, kcols], live, ng_ref, og_ref, vcols, s_scr, h,
                  chunk=chunk, sub=sub, n_live=n_live)

    @pl.when(c == n_chunks - 1)
    def _():
        sfin_ref[0] = s_scr[...]


def _gla_head(q, k, v, r, bc, live, ng_ref, og_ref, vcols, s_scr, h, *, chunk, sub, n_live):
    if live is not None:
        k = jnp.where(live, k, 0.0)
        v = jnp.where(live, v, 0.0)
    s = s_scr[h]
    o = jnp.dot((q * jnp.exp(bc)).astype(BF16), s.astype(BF16), preferred_element_type=F32)
    sub_row = lax.broadcasted_iota(jnp.int32, (sub, 1), 0)
    o_parts = []
    for blk in range(chunk // sub):
        lo = blk * sub
        b_i = bc[lo:lo + sub]
        q_i = q[lo:lo + sub]
        o_i = o[lo:lo + sub]
        if blk > 0:
            b_ref = bc[lo - 1:lo]
            q_t = q_i * jnp.exp(b_i - b_ref)
            k_t = k[:lo] * jnp.exp(b_ref - bc[:lo])
            att = lax.dot_general(q_t.astype(BF16), k_t.astype(BF16), (((1,), (1,)), ((), ())),
                                  preferred_element_type=F32)
            o_i = o_i + jnp.dot(att.astype(BF16), v[:lo].astype(BF16), preferred_element_type=F32)
        groups = [(a, min(a + 8, sub)) for a in range(0, sub, 8)]
        o_g = [o_i[a:b] for a, b in groups]
        for j in range(min(sub, max(n_live - lo, 0))):
            b_j = bc[lo + j:lo + j + 1]
            k_j = k[lo + j:lo + j + 1]
            v_j = v[lo + j:lo + j + 1]
            for gi, (a, b) in enumerate(groups):
                if b <= j:
                    continue
                d = b_i[a:b] - b_j
                if a < j:
                    d = jnp.where(sub_row[a:b] >= j, d, NEG_INF)
                w_col = jnp.sum(q_i[a:b] * k_j * jnp.exp(d), axis=-1, keepdims=True)
                o_g[gi] = o_g[gi] + w_col * v_j
        o_parts.extend(o_g)
    o = jnp.concatenate(o_parts, axis=0) if len(o_parts) > 1 else o_parts[0]

    o = o * lax.rsqrt(jnp.mean(o * o, axis=-1, keepdims=True) + LN_EPS) * ng_ref[...]
    og_ref[0, :, vcols] = (o * (r * jax.nn.sigmoid(r))).astype(og_ref.dtype)

    b_last = bc[chunk - 1:chunk]
    k_d = k * jnp.exp(b_last - bc)
    decay = jnp.transpose(jnp.broadcast_to(jnp.exp(b_last), (GLA_DK, GLA_DK)))
    kv = lax.dot_general(k_d.astype(BF16), v.astype(BF16), (((0,), (0,)), ((), ())),
                         preferred_element_type=F32)
    s_scr[h] = jnp.concatenate([decay] * (GLA_DV // GLA_DK), axis=1) * s + kv


def _gla_scan(proj, s0, wg2_pad, b_g, norm_g, batch, seq_len):
    cols = proj.shape[1]
    if seq_len >= 64:
        chunk, sub = 64, 16
    else:
        chunk = sub = 16
    padded = -(-seq_len // chunk) * chunk
    p3 = proj.reshape(batch, seq_len, cols)
    if padded != seq_len:
        p3 = jnp.pad(p3, ((0, 0), (0, padded - seq_len), (0, 0)))
    n_chunks = padded // chunk
    dq = GLA_HEADS * GLA_DK
    dv = GLA_HEADS * GLA_DV
    k0, v0, r0, g0 = 1, 2 * dq // dv, (2 * dq + dv) // dv, (2 * dq + 2 * dv) // LANES
    state_spec = pl.BlockSpec((1, GLA_HEADS, GLA_DK, GLA_DV), lambda b, c: (b, 0, 0, 0))
    og, s_fin = pl.pallas_call(
        functools.partial(_gla_kernel, chunk=chunk, sub=sub, seq_len=seq_len, padded_len=padded),
        grid=(batch, n_chunks),
        in_specs=[pl.BlockSpec((1, chunk, dq), lambda b, c: (b, c, 0)),
                  pl.BlockSpec((1, chunk, dq), lambda b, c: (b, c, k0)),
                  pl.BlockSpec((1, chunk, dv), lambda b, c: (b, c, v0)),
                  pl.BlockSpec((1, chunk, dv), lambda b, c: (b, c, r0)),
                  pl.BlockSpec((1, chunk, LANES), lambda b, c: (b, c, g0)),
                  pl.BlockSpec((LANES, dq), lambda b, c: (0, 0)),
                  pl.BlockSpec((1, dq), lambda b, c: (0, 0)),
                  pl.BlockSpec((1, GLA_DV), lambda b, c: (0, 0)),
                  state_spec],
        out_specs=[pl.BlockSpec((1, chunk, dv), lambda b, c: (b, c, 0)), state_spec],
        out_shape=[jax.ShapeDtypeStruct((batch, padded, dv), BF16),
                   jax.ShapeDtypeStruct((batch, GLA_HEADS, GLA_DK, GLA_DV), F32)],
        scratch_shapes=[pltpu.VMEM((GLA_HEADS, GLA_DK, GLA_DV), F32)],
        compiler_params=_cparams("parallel", "arbitrary"),
        name="gla_scan",
    )(p3, p3, p3, p3, p3, wg2_pad, b_g.reshape(1, dq), norm_g.reshape(1, GLA_DV), s0)
    og = og[:, :seq_len].reshape(batch * seq_len, GLA_HEADS * GLA_DV)
    return og, s_fin


def _gla_layer(x, mods, s0, w_in_pad, wg2_pad, b_g, norm_g, w_o, ln_g, ln_b, batch, seq_len):
    shift, scale, gate = mods
    (proj,) = _proj(x, scale, shift, w_in_pad, seq_len, [(F32, 0, w_in_pad.shape[1])])
    og, s_fin = _gla_scan(proj, s0, wg2_pad, b_g, norm_g, batch, seq_len)
    return _out_ln(og, w_o, x, gate, ln_g, ln_b, seq_len), s_fin


def _top_values(s, count, want_rank=False):
    vals = []
    rank = jnp.full(s.shape, float(count), F32) if want_rank else None
    for r in range(count):
        m = jnp.max(s, axis=0, keepdims=True)
        vals.append(m)
        hit = s == m
        if want_rank:
            rank = jnp.where(hit, float(r), rank)
        if r + 1 < count:
            s = jnp.where(hit, NEG_INF, s)
    return (vals, rank) if want_rank else vals


def _peer_select(q_scr, keys_ref, c_scr, e1_scr, r2_scr, e2_scr, head):
    row0 = pl.multiple_of(head * PEER_DKEY, PEER_DKEY)
    half = PEER_DKEY // 2
    s1_all = jnp.dot(keys_ref[0], q_scr[pl.ds(row0, half), :],
                     precision=lax.Precision.HIGHEST, preferred_element_type=F32)
    s2_all = jnp.dot(keys_ref[1], q_scr[pl.ds(row0 + half, half), :],
                     precision=lax.Precision.HIGHEST, preferred_element_type=F32)
    width = min(LANES, s1_all.shape[1])
    for g in range(s1_all.shape[1] // width):
        cols = slice(g * width, (g + 1) * width)
        _peer_select_group(s1_all[:, cols], s2_all[:, cols], c_scr, e1_scr, r2_scr, e2_scr, head, cols)


def _peer_select_group(s1, s2, c_scr, e1_scr, r2_scr, e2_scr, head, cols):
    a = _top_values(s1, PEER_TOPK)
    b, rank2 = _top_values(s2, PEER_TOPK, want_rank=True)
    s1 = s1 - a[0]
    s2 = s2 - b[0]
    a = [v - a[0] for v in a]
    b = [v - b[0] for v in b]
    cand = [a[i] + b[j] for i in range(PEER_TOPK) for j in range(PEER_TOPK) if (i + 1) * (j + 1) <= PEER_TOPK]
    pad = (-len(cand)) % 8
    cand = jnp.concatenate(cand + [jnp.full_like(cand[0], NEG_INF)] * pad, axis=0)
    top = _top_values(cand, PEER_TOPK)
    cut = top[-1]
    z = jnp.zeros_like(cut)
    for v in top:
        z = z + jnp.exp(v)
    count = jnp.zeros_like(s1)
    for bj in b:
        count = count + jnp.where(s1 + bj >= cut, 1.0, 0.0)
    c_scr[head, :, cols] = count
    e1_scr[head, :, cols] = jnp.exp(s1) / z
    r2_scr[head, :, cols] = rank2.astype(BF16)
    e2_scr[head, :, cols] = jnp.exp(s2).astype(BF16)


def _peer_kernel(x_ref, sc_ref, sh_ref, gate_ref, wq_ref, keys_ref, u_ref, vt_ref, lng_ref, lnb_ref,
                 o_ref, ht_scr, q_scr, c_scr, e1_scr, r2_scr, e2_scr, acc_scr, a_scr, p_scr, *, te, eb):
    j = pl.program_id(1)

    @pl.when(j == 0)
    def _():
        h = x_ref[...] * (1.0 + sc_ref[0]) + sh_ref[0]
        ht_scr[...] = jnp.transpose(h).astype(BF16)
        q_scr[...] = jnp.dot(wq_ref[...], ht_scr[...], preferred_element_type=F32)

        def head_body(head, carry):
            _peer_select(q_scr, keys_ref, c_scr, e1_scr, r2_scr, e2_scr, head)
            return carry
        lax.fori_loop(0, PEER_HEADS, head_body, 0, unroll=2)
        acc_scr[...] = jnp.zeros_like(acc_scr)

    def activations(e):
        a_scr[e % 2] = jnp.dot(u_ref[e * eb:(e + 1) * eb, :], ht_scr[...],
                               preferred_element_type=F32)

    def weigh(e):
        key1 = (j * te + e * eb) // PEER_NKEYS
        for i1 in range(eb // PEER_NKEYS):
            rows = slice(i1 * PEER_NKEYS, (i1 + 1) * PEER_NKEYS)
            w = None
            for head in range(PEER_HEADS):
                row = pl.ds(key1 + i1, 1)
                keep = r2_scr[head] < c_scr[head, row, :].astype(BF16)
                val = jnp.where(keep, e2_scr[head] * e1_scr[head, row, :].astype(BF16), jnp.zeros((), BF16))
                w = val if w is None else w + val
            a_half = a_scr[e % 2, rows, :]
            gelu = a_half * (1.0 + lax.erf(a_half * (2.0 ** 0.5)))
            p_scr[e % 2, rows, :] = w * gelu.astype(BF16)

    def combine(e):
        acc_scr[...] += jnp.dot(vt_ref[:, e * eb:(e + 1) * eb], p_scr[e % 2],
                                preferred_element_type=F32)

    n_blocks = te // eb
    activations(0)
    for e in range(n_blocks):
        if e + 1 < n_blocks:
            activations(e + 1)
        weigh(e)
        combine(e)

    @pl.when(j == pl.num_programs(1) - 1)
    def _():
        out = jnp.transpose(acc_scr[...])
        z = DN_ALPHA * x_ref[...] + (1.0 + gate_ref[0]) * out
        o_ref[...] = _layernorm_rows(z, lng_ref[...], lnb_ref[...])


def _peer_layer(x, mods, wq_t, keys, u_half, v_t, ln_g, ln_b, seq_len, tm_want=512, te=2048, eb=512):
    shift, scale, gate = mods
    n, d = x.shape
    tm = _row_tile(n, tm_want)
    n_exp = u_half.shape[0]
    sc, sc_blk, sc_idx = _mod_operand(scale, seq_len, tm)
    sh, sh_blk, sh_idx = _mod_operand(shift, seq_len, tm)
    gt, gt_blk, gt_idx = _mod_operand(gate, seq_len, tm)
    hq = PEER_HEADS * PEER_DKEY
    row = lambda f: (lambda i, j: f(i))
    return pl.pallas_call(
        functools.partial(_peer_kernel, te=te, eb=eb),
        grid=(n // tm, n_exp // te),
        in_specs=[pl.BlockSpec((tm, d), lambda i, j: (i, 0)),
                  pl.BlockSpec(sc_blk, row(sc_idx)),
                  pl.BlockSpec(sh_blk, row(sh_idx)),
                  pl.BlockSpec(gt_blk, row(gt_idx)),
                  pl.BlockSpec((hq, d), lambda i, j: (0, 0)),
                  pl.BlockSpec((2, PEER_NKEYS, PEER_DKEY // 2), lambda i, j: (0, 0, 0)),
                  pl.BlockSpec((te, d), lambda i, j: (j, 0)),
                  pl.BlockSpec((d, te), lambda i, j: (0, j)),
                  pl.BlockSpec((1, d), lambda i, j: (0, 0)),
                  pl.BlockSpec((1, d), lambda i, j: (0, 0))],
        out_specs=pl.BlockSpec((tm, d), lambda i, j: (i, 0)),
        out_shape=jax.ShapeDtypeStruct((n, d), F32),
        scratch_shapes=[pltpu.VMEM((d, tm), BF16),
                        pltpu.VMEM((hq, tm), F32),
                        pltpu.VMEM((PEER_HEADS, PEER_NKEYS, tm), F32),
                        pltpu.VMEM((PEER_HEADS, PEER_NKEYS, tm), F32),
                        pltpu.VMEM((PEER_HEADS, PEER_NKEYS, tm), BF16),
                        pltpu.VMEM((PEER_HEADS, PEER_NKEYS, tm), BF16),
                        pltpu.VMEM((d, tm), F32),
                        pltpu.VMEM((2, eb, tm), F32),
                        pltpu.VMEM((2, eb, tm), BF16)],
        compiler_params=_cparams("parallel", "arbitrary"),
        name="peer",
    )(x, sc, sh, gt, wq_t, keys, u_half, v_t, ln_g.reshape(1, d), ln_b.reshape(1, d))


def _block_mean_kernel(*refs, n_parts):
    o_ref = refs[-1]
    total = None
    rows = 0
    for p_ref in refs[-1 - n_parts:-1]:
        part = jnp.sum(p_ref[0].astype(F32), axis=0, keepdims=True)
        total = part if total is None else total + part
        rows += p_ref.shape[1]
    o_ref[0] = total * (1.0 / rows)


def _block_means_dense(k, batch, seq_len):
    w = k.shape[1]
    nb = seq_len // MOBA_BLOCK
    out = pl.pallas_call(
        functools.partial(_block_mean_kernel, n_parts=1),
        grid=(batch * nb,),
        in_specs=[pl.BlockSpec((1, MOBA_BLOCK, w), lambda i: (i, 0, 0))],
        out_specs=pl.BlockSpec((1, 1, w), lambda i: (i, 0, 0)),
        out_shape=jax.ShapeDtypeStruct((batch * nb, 1, w), F32),
        compiler_params=_cparams("parallel"),
        name="block_means",
    )(k.reshape(batch * nb, MOBA_BLOCK, w))
    return out.reshape(batch, nb, w)


def _moba_prompt_kernel(qi_ref, kv0_ref, kv1_ref, q_ref, k0_ref, v0_ref, k1_ref, v1_ref, mean_ref, o_ref,
                        m_scr, acc_scr, qa_scr):
    t = pl.program_id(1)
    qi = qi_ref[t]
    kv0 = kv0_ref[t]
    kv1 = kv1_ref[t]
    own = kv0 == qi
    blk = MOBA_BLOCK
    masked = 1e30
    row = lax.broadcasted_iota(jnp.int32, (blk, blk), 0)
    col = lax.broadcasted_iota(jnp.int32, (blk, blk), 1)
    causal_bias = jnp.where(own, jnp.where(col <= row, 0.0, NEG_INF), 0.0)
    gcol_i = lax.broadcasted_iota(jnp.int32, (blk, LANES), 1)
    head_cols = [slice(h * HEAD_DIM, (h + 1) * HEAD_DIM) for h in range(MOBA_HEADS)]

    @pl.when(own)
    def _():
        gcol = gcol_i.astype(F32)
        n_past = qi.astype(F32)
        for h, cs in enumerate(head_cols):
            g = lax.dot_general(q_ref[0, :, cs].astype(F32), mean_ref[0, :, cs], (((1,), (1,)), ((), ())),
                                precision=lax.Precision.HIGHEST, preferred_element_type=F32)
            g = jnp.where(gcol < n_past, g, NEG_INF)
            bias = jnp.where(gcol == n_past, 0.0, -masked)
            for r in range(MOBA_TOPK):
                best = jnp.max(g, axis=-1, keepdims=True)
                pick = jnp.min(jnp.where(g == best, gcol, float(LANES)), axis=-1, keepdims=True)
                hit = gcol == pick
                bias = jnp.where(r < qi, jnp.where(hit, 0.0, bias), bias)
                g = jnp.where(hit, NEG_INF, g)
            qa_scr[h] = jnp.concatenate([q_ref[0, :, cs], bias.astype(BF16)], axis=1)
            m_scr[h] = jnp.full((blk, LANES), NEG_INF, F32)
            acc_scr[h] = jnp.zeros((blk, 2 * HEAD_DIM), F32)

    onehot0 = jnp.where(gcol_i == kv0, 1.0, 0.0).astype(BF16)
    onehot1 = jnp.where(gcol_i == kv1, 1.0, 0.0).astype(BF16)
    absent1 = jnp.where(kv1 < 0, -masked, 0.0)
    ones = jnp.ones((blk, HEAD_DIM), BF16)
    twice = lambda a: jnp.concatenate([a, a], axis=1)
    nt = (((1,), (1,)), ((), ()))
    for h, cs in enumerate(head_cols):
        qa = qa_scr[h]
        s0 = lax.dot_general(qa, jnp.concatenate([k0_ref[0, :, cs], onehot0], axis=1), nt,
                             preferred_element_type=F32) + causal_bias
        s1 = lax.dot_general(qa, jnp.concatenate([k1_ref[0, :, cs], onehot1], axis=1), nt,
                             preferred_element_type=F32) + absent1
        m_old = m_scr[h]
        m_new = jnp.maximum(m_old, jnp.maximum(jnp.max(s0, axis=-1, keepdims=True),
                                               jnp.max(s1, axis=-1, keepdims=True)))
        alpha = jnp.exp(m_old - m_new)
        p0 = jnp.exp(s0 - twice(m_new)).astype(BF16)
        p1 = jnp.exp(s1 - twice(m_new)).astype(BF16)
        acc_scr[h] = (twice(alpha) * acc_scr[h]
                      + jnp.dot(p0, jnp.concatenate([v0_ref[0, :, cs], ones], axis=1), preferred_element_type=F32)
                      + jnp.dot(p1, jnp.concatenate([v1_ref[0, :, cs], ones], axis=1), preferred_element_type=F32))
        m_scr[h] = m_new

    @pl.when(kv0 <= 1)
    def _():
        for h, cs in enumerate(head_cols):
            acc = acc_scr[h]
            o_ref[0, :, cs] = (acc[:, :HEAD_DIM] / acc[:, HEAD_DIM:]).astype(o_ref.dtype)


def _moba_prompt(q, k, v, means, batch, seq_len):
    w = q.shape[1]
    nb = seq_len // MOBA_BLOCK
    assert nb <= LANES
    means = jnp.pad(means, ((0, 0), (0, LANES - nb), (0, 0)))
    steps = [(qi, kv, kv - 1) for qi in range(nb) for kv in range(qi, -1, -2)]
    qi_tab, kv0_tab, kv1_tab = (jnp.asarray(col, jnp.int32) for col in zip(*steps))
    r3 = lambda a: a.reshape(batch, seq_len, w)
    q_spec = pl.BlockSpec((1, MOBA_BLOCK, w), lambda b, t, qt, k0, k1: (b, qt[t], 0))
    kv0_spec = pl.BlockSpec((1, MOBA_BLOCK, w), lambda b, t, qt, k0, k1: (b, k0[t], 0))
    kv1_spec = pl.BlockSpec((1, MOBA_BLOCK, w), lambda b, t, qt, k0, k1: (b, jnp.maximum(k1[t], 0), 0))
    out = pl.pallas_call(
        _moba_prompt_kernel,
        grid_spec=pltpu.PrefetchScalarGridSpec(
            num_scalar_prefetch=3,
            grid=(batch, len(steps)),
            in_specs=[q_spec, kv0_spec, kv0_spec, kv1_spec, kv1_spec,
                      pl.BlockSpec((1, LANES, w), lambda b, t, qt, k0, k1: (b, 0, 0))],
            out_specs=q_spec,
            scratch_shapes=[pltpu.VMEM((MOBA_HEADS, MOBA_BLOCK, LANES), F32),
                            pltpu.VMEM((MOBA_HEADS, MOBA_BLOCK, 2 * HEAD_DIM), F32),
                            pltpu.VMEM((MOBA_HEADS, MOBA_BLOCK, HEAD_DIM + LANES), BF16)]),
        out_shape=jax.ShapeDtypeStruct((batch, seq_len, w), BF16),
        compiler_params=_cparams("parallel", "arbitrary"),
        name="moba_prompt",
    )(qi_tab, kv0_tab, kv1_tab, r3(q), r3(k), r3(v), r3(k), r3(v), means)
    return out.reshape(batch * seq_len, w)


def _moba_decode_kernel(pt_ref, q_ref, kn_ref, vn_ref, *rest, n_new, n_past, ppb):
    k_pages = rest[:ppb]
    v_pages = rest[ppb:2 * ppb]
    o_ref, g_scr, m_scr, l_scr, acc_scr = rest[2 * ppb:]
    j = pl.program_id(1)
    nq = MOBA_HEADS * n_new
    head_bits = MOBA_HEADS.bit_length() - 1
    new_bits = n_new.bit_length() - 1
    q = q_ref[0]
    qb = q.astype(BF16)

    def same_head(n_cols):
        lane_head = lax.broadcasted_iota(jnp.int32, (nq, n_cols), 1) & (MOBA_HEADS - 1)
        row_head = lax.broadcasted_iota(jnp.int32, (nq, n_cols), 0) >> new_bits
        return lane_head == row_head

    def scores(keys):
        return lax.dot_general(qb, keys, (((1,), (1,)), ((), ())), preferred_element_type=F32)

    page_len = k_pages[0].shape[1]
    page_rows = page_len * MOBA_HEADS
    pages_per_block = MOBA_BLOCK // page_len
    head_bias = jnp.where(same_head(page_rows), 0.0, NEG_INF)
    for bi in range(ppb // pages_per_block):
        block = j * (ppb // pages_per_block) + bi
        k_blk = k_pages[bi * pages_per_block:(bi + 1) * pages_per_block]
        v_blk = v_pages[bi * pages_per_block:(bi + 1) * pages_per_block]
        key_sum = None
        for k_ref in k_blk:
            part = jnp.sum(k_ref[0], axis=0)
            key_sum = part if key_sum is None else key_sum + part
        gate = lax.dot_general(q, key_sum * (1.0 / MOBA_BLOCK), (((1,), (1,)), ((), ())),
                               precision=lax.Precision.HIGHEST, preferred_element_type=F32)
        g_scr[block] = jnp.sum(jnp.where(same_head(MOBA_HEADS), gate, 0.0), axis=-1, keepdims=True)
        s_pages = [scores(k_ref[0].reshape(page_rows, HEAD_DIM).astype(BF16)) + head_bias for k_ref in k_blk]
        m = None
        for s in s_pages:
            s_max = jnp.max(s, axis=-1, keepdims=True)
            m = s_max if m is None else jnp.maximum(m, s_max)
        l = None
        acc = None
        for s, v_ref in zip(s_pages, v_blk):
            p = jnp.exp(s - m)
            p_sum = jnp.sum(p, axis=-1, keepdims=True)
            pv = jnp.dot(p.astype(BF16), v_ref[0].reshape(page_rows, HEAD_DIM).astype(BF16),
                         preferred_element_type=F32)
            l = p_sum if l is None else l + p_sum
            acc = pv if acc is None else acc + pv
        m_scr[block] = m
        l_scr[block] = l
        acc_scr[block] = acc

    @pl.when(j == pl.num_programs(1) - 1)
    def _():
        s = scores(kn_ref[0].astype(BF16))
        key_t = lax.broadcasted_iota(jnp.int32, s.shape, 1) >> head_bits
        qry_r = lax.broadcasted_iota(jnp.int32, s.shape, 0) & (n_new - 1)
        s = jnp.where(same_head(s.shape[1]), jnp.where(key_t <= qry_r, s, NEG_INF), NEG_INF)
        m_own = jnp.max(s, axis=-1, keepdims=True)
        p = jnp.exp(s - m_own)
        l_own = jnp.sum(p, axis=-1, keepdims=True)
        acc_own = jnp.dot(p.astype(BF16), vn_ref[0].astype(BF16), preferred_element_type=F32)

        k_sel = min(MOBA_TOPK, n_past)
        gates = [g_scr[b] for b in range(n_past)]
        weights = []
        m_all = m_own
        for b in range(n_past):
            place = jnp.zeros_like(gates[b])
            for o in range(n_past):
                if o != b:
                    ahead = (gates[o] >= gates[b]) if o < b else (gates[o] > gates[b])
                    place = place + jnp.where(ahead, 1.0, 0.0)
            weights.append(jnp.where(place < k_sel, 0.0, NEG_INF))
            m_all = jnp.maximum(m_all, m_scr[b] + weights[b])
        w_own = jnp.exp(m_own - m_all)
        l_all = w_own * l_own
        acc_all = w_own * acc_own
        for b in range(n_past):
            w_b = jnp.exp(m_scr[b] + weights[b] - m_all)
            l_all = l_all + w_b * l_scr[b]
            acc_all = acc_all + w_b * acc_scr[b]
        o_ref[0] = acc_all / l_all


def _moba_decode(q, k_new, v_new, cache_k, cache_v, page_table_flat, batch, n_new, pages_per_seq):
    _, page, heads, hd = cache_k.shape
    assert heads & (heads - 1) == 0 and n_new & (n_new - 1) == 0
    n_past = pages_per_seq * page // MOBA_BLOCK
    blocks_per_step = next(c for c in (4, 2, 1) if n_past % c == 0)
    ppb = (MOBA_BLOCK // page) * blocks_per_step
    nq = heads * n_new
    q_hq = q.reshape(batch, n_new, heads, hd).transpose(0, 2, 1, 3).reshape(batch, nq, hd)
    kv_rows = lambda a: a.reshape(batch, nq, hd)
    seq_spec = pl.BlockSpec((1, nq, hd), lambda b, j, pt: (b, 0, 0))
    page_spec = lambda part: pl.BlockSpec(
        (1, page, heads, hd), lambda b, j, pt: (pt[b * pages_per_seq + j * ppb + part], 0, 0, 0))
    out = pl.pallas_call(
        functools.partial(_moba_decode_kernel, n_new=n_new, n_past=n_past, ppb=ppb),
        grid_spec=pltpu.PrefetchScalarGridSpec(
            num_scalar_prefetch=1,
            grid=(batch, n_past // blocks_per_step),
            in_specs=[seq_spec, seq_spec, seq_spec] + [page_spec(p) for p in range(ppb)] * 2,
            out_specs=seq_spec,
            scratch_shapes=[pltpu.VMEM((n_past, nq, 1), F32),
                            pltpu.VMEM((n_past, nq, 1), F32),
                            pltpu.VMEM((n_past, nq, 1), F32),
                            pltpu.VMEM((n_past, nq, hd), F32)]),
        out_shape=jax.ShapeDtypeStruct((batch, nq, hd), F32),
        compiler_params=_cparams("parallel", "arbitrary"),
        name="moba_decode",
    )(page_table_flat, q_hq, kv_rows(k_new), kv_rows(v_new),
      *([cache_k] * ppb), *([cache_v] * ppb))
    return out.reshape(batch, heads, n_new, hd).transpose(0, 2, 1, 3).reshape(batch * n_new, heads * hd)


def _rope_tables(pos):
    half = HEAD_DIM // 2
    inv = ROPE_THETA ** (-jnp.arange(half, dtype=F32) / half)
    ang = pos.astype(F32)[:, None] * inv[None, :]
    cos, sin = jnp.cos(ang), jnp.sin(ang)
    return jnp.concatenate([cos, cos], axis=-1), jnp.concatenate([-sin, sin], axis=-1)


def _trunk(x, mods, mods_kv, s0, past, p, batch, seq_len):
    pos0 = 0 if past is None else past[3] * past[0].shape[1]
    pos = pos0 + jnp.arange(seq_len, dtype=jnp.int32)
    cos, sin = _rope_tables(pos)
    n = batch * seq_len
    if seq_len % _row_tile(n, 512):
        cos, sin = jnp.tile(cos, (batch, 1)), jnp.tile(sin, (batch, 1))
    gla_fin = []
    k_new = v_new = None
    dm = MOBA_HEADS * HEAD_DIM
    for layer in range(DEPTH):
        shift, scale, gate = mods[layer][0]
        if layer < N_A:
            x, s_fin = _gla_layer(x, mods[layer][0], s0[layer], p["gla_w_in"][layer], p["gla_w_g2"][layer],
                                  p["gla_b_g"][layer], p["gla_norm_g"][layer], p["gla_w_o"][layer],
                                  p["ln_g"][layer, 0], p["ln_b"][layer, 0], batch, seq_len)
            gla_fin.append(s_fin)
        else:
            jb = layer - N_A
            (q,) = _proj(x, scale, shift, p["moba_w_q"][jb], seq_len, [(BF16 if past is None else F32, 0, dm)],
                         rope=(cos, sin), rope_heads=MOBA_HEADS, out_scale=HEAD_DIM ** -0.5)
            if past is None:
                o = _moba_prompt(q, k_bf16, v_bf16, means, batch, seq_len)
            else:
                o = _moba_decode(q, k_new, v_new, past[0], past[1], past[2], batch, seq_len,
                                 past[3]).astype(BF16)
            x = _out_ln(o, p["moba_w_o"][jb], x, gate, p["ln_g"][layer, 0], p["ln_b"][layer, 0], seq_len)
        x = _peer_layer(x, mods[layer][1], p["peer_w_q"][layer], p["peer_keys"][layer], p["peer_u"][layer],
                        p["peer_v"][layer], p["ln_g"][layer, 1], p["ln_b"][layer, 1], seq_len)
        if layer == N_A - 1:
            kv_shift, kv_scale = mods_kv
            kv_cols = [(F32, 0, dm), (F32, dm, 2 * dm)]
            if past is None:
                kv_cols += [(BF16, 0, dm), (BF16, dm, 2 * dm)]
            kv = _proj(x, kv_scale, kv_shift, p["w_kv"], seq_len, kv_cols, rope=(cos, sin),
                       rope_heads=MOBA_HEADS)
            k_new, v_new = kv[:2]
            if past is None:
                k_bf16, v_bf16 = kv[2:]
                means = _block_means_dense(k_new, batch, seq_len)
    return x, jnp.stack(gla_fin), k_new, v_new


def kernel(x_prompt, x_sample, c_prompt, c_sample, state_gla, cache_k, cache_v, page_table,
           w_ada, b_ada, ln_g, ln_b, gla_w_in, gla_w_g2, gla_b_g, gla_norm_g, gla_w_o,
           w_ada_kv, b_ada_kv, w_kv, moba_w_q, moba_w_o, peer_w_q, peer_keys, peer_u, peer_v):
    b_p, l_p, d = x_prompt.shape
    b_s, l_s, _ = x_sample.shape
    gla_cols = -(-gla_w_in.shape[-1] // LANES) * LANES
    p = {
        "ln_g": ln_g, "ln_b": ln_b,
        "gla_w_in": jnp.pad(gla_w_in, ((0, 0), (0, 0), (0, gla_cols - gla_w_in.shape[-1]))).astype(BF16),
        "gla_w_g2": jnp.pad(gla_w_g2, ((0, 0), (0, LANES - GLA_GATE_RANK), (0, 0))),
        "gla_b_g": gla_b_g, "gla_norm_g": gla_norm_g,
        "gla_w_o": gla_w_o.astype(BF16),
        "w_kv": w_kv.astype(BF16),
        "moba_w_q": moba_w_q.astype(BF16), "moba_w_o": moba_w_o.astype(BF16),
        "peer_w_q": jnp.swapaxes(peer_w_q, 1, 2).astype(BF16),
        "peer_keys": peer_keys,
        "peer_u": (0.5 * peer_u).astype(BF16),
        "peer_v": jnp.swapaxes(peer_v, 1, 2).astype(BF16),
    }
    c_all = jnp.concatenate([c_prompt, c_sample], axis=0)
    rows = c_all.shape[0]
    c_all = jnp.pad(c_all, ((0, (-rows) % 8), (0, 0)))
    m_all = _ada_all(c_all, w_ada.reshape(DEPTH * 2, d, 3 * d), b_ada.reshape(DEPTH * 2, 1, 3 * d))
    m_kv = _ada_all(c_all, w_ada_kv[None], b_ada_kv[None, None])[0]

    def group_mods(lo, hi):
        mods = [[tuple(m_all[layer * 2 + sub, lo:hi, i * d:(i + 1) * d] for i in range(3))
                 for sub in range(2)] for layer in range(DEPTH)]
        return mods, (m_kv[lo:hi, :d], m_kv[lo:hi, d:])

    mods_p, kv_p = group_mods(0, b_p)
    mods_s, kv_s = group_mods(b_p, b_p + b_s)

    zero_state = jnp.zeros((N_A, b_p) + state_gla.shape[2:], F32)
    y_p, gla_p, k_p, v_p = _trunk(x_prompt.reshape(b_p * l_p, d), mods_p, kv_p, zero_state, None, p, b_p, l_p)

    past = (cache_k, cache_v, page_table.reshape(-1).astype(jnp.int32), page_table.shape[1])
    y_s, gla_s, k_s, v_s = _trunk(x_sample.reshape(b_s * l_s, d), mods_s, kv_s, state_gla, past, p, b_s, l_s)

    heads = lambda a, b, l: a.reshape(b, l, MOBA_HEADS, HEAD_DIM)
    return (y_p.reshape(b_p, l_p, d), y_s.reshape(b_s, l_s, d), gla_p, gla_s,
            heads(k_p, b_p, l_p), heads(v_p, b_p, l_p), heads(k_s, b_s, l_s), heads(v_s, b_s, l_s))
```

```python
import functools
import math

import jax
import jax.numpy as jnp
from jax import lax
from jax.experimental import pallas as pl
from jax.experimental.pallas import tpu as pltpu

F32 = jnp.float32
BF16 = jnp.bfloat16

D_MODEL = 1024
DEPTH = 4
N_A = DEPTH // 2
N_B = DEPTH - N_A
GLA_HEADS = 4
GLA_DK = D_MODEL // 2 // GLA_HEADS
GLA_DV = D_MODEL // GLA_HEADS
GLA_GATE_RANK = 16
GLA_TAU = 16.0
MOBA_HEADS = 8
HEAD_DIM = D_MODEL // MOBA_HEADS
MOBA_BLOCK = 256
MOBA_TOPK = 3
ROPE_THETA = 10000.0
PEER_HEADS = 8
PEER_NKEYS = 128
PEER_EXPERTS = PEER_NKEYS * PEER_NKEYS
PEER_DKEY = 256
PEER_TOPK = 16
DN_ALPHA = (2.0 * DEPTH) ** 0.25
LN_EPS = 1e-5

LANES = 128
VMEM_LIMIT = 56 * 1024 * 1024
NEG_INF = float("-inf")


def _cparams(*sem):
    return pltpu.CompilerParams(dimension_semantics=sem, vmem_limit_bytes=VMEM_LIMIT)


def _row_tile(n_rows, want):
    t = min(want, n_rows)
    while n_rows % t:
        t //= 2
    return t


def _mod_operand(m, seq_len, tm):
    d = m.shape[-1]
    if seq_len % tm == 0:
        per = seq_len // tm
        return m[:, None, :], (1, 1, d), (lambda i: (i // per, 0, 0))
    rows = jnp.repeat(m, seq_len, axis=0)
    return rows.reshape(rows.shape[0] // tm, tm, d), (1, tm, d), (lambda i: (i, 0, 0))


def _layernorm_rows(z, g, b):
    mu = jnp.mean(z, axis=-1, keepdims=True)
    zc = z - mu
    var = jnp.mean(zc * zc, axis=-1, keepdims=True)
    return zc * lax.rsqrt(var + LN_EPS) * g + b


def _ada_kernel(c_ref, w_ref, b_ref, o_ref):
    c = c_ref[...]
    s = c * jax.nn.sigmoid(c)
    o_ref[0] = jnp.dot(s.astype(BF16), w_ref[0].astype(BF16), preferred_element_type=F32) + b_ref[0]


def _ada_all(c, w, b):
    g, d, n = w.shape
    r = c.shape[0]
    tn = 1024
    return pl.pallas_call(
        _ada_kernel,
        grid=(g, n // tn),
        in_specs=[pl.BlockSpec((r, d), lambda gi, j: (0, 0)),
                  pl.BlockSpec((1, d, tn), lambda gi, j: (gi, 0, j)),
                  pl.BlockSpec((1, 1, tn), lambda gi, j: (gi, 0, j))],
        out_specs=pl.BlockSpec((1, r, tn), lambda gi, j: (gi, 0, j)),
        out_shape=jax.ShapeDtypeStruct((g, r, n), F32),
        compiler_params=_cparams("parallel", "parallel"),
        name="ada",
    )(c, w, b)


def _rope_cols(y, cos, sin, heads):
    outs = []
    for hh in range(heads):
        yh = y[:, hh * HEAD_DIM:(hh + 1) * HEAD_DIM]
        outs.append(yh * cos + pltpu.roll(yh, HEAD_DIM // 2, axis=1) * sin)
    return outs


def _proj_kernel(x_ref, sc_ref, sh_ref, w_ref, *rest, rope_heads, out_cols, out_scale):
    if rope_heads:
        cos_ref, sin_ref = rest[:2]
        outs = rest[2:]
    else:
        outs = rest
    h = x_ref[...] * (1.0 + sc_ref[0]) + sh_ref[0]
    y = jnp.dot(h.astype(BF16), w_ref[...], preferred_element_type=F32)
    if out_scale != 1.0:
        y = y * out_scale
    pieces = []
    if rope_heads:
        parts = _rope_cols(y, cos_ref[...], sin_ref[...], rope_heads)
        pieces += [(hh * HEAD_DIM, p) for hh, p in enumerate(parts)]
    plain = rope_heads * HEAD_DIM
    for lo, hi in sorted({(max(lo, plain), hi) for _, lo, hi in out_cols if hi > plain}):
        pieces.append((lo, y[:, lo:hi]))
    for o, (_, lo, hi) in zip(outs, out_cols):
        for start, vals in pieces:
            if lo <= start and start + vals.shape[1] <= hi:
                o[:, start - lo:start - lo + vals.shape[1]] = vals.astype(o.dtype)


def _proj(x, scale, shift, w_bf16, seq_len, out_cols, rope=None, rope_heads=0, out_scale=1.0, tm_want=512):
    n, d = x.shape
    n_out = w_bf16.shape[1]
    tm = _row_tile(n, tm_want)
    sc, sc_blk, sc_idx = _mod_operand(scale, seq_len, tm)
    sh, sh_blk, sh_idx = _mod_operand(shift, seq_len, tm)
    in_specs = [pl.BlockSpec((tm, d), lambda i: (i, 0)),
                pl.BlockSpec(sc_blk, sc_idx),
                pl.BlockSpec(sh_blk, sh_idx),
                pl.BlockSpec((d, n_out), lambda i: (0, 0))]
    args = [x, sc, sh, w_bf16]
    if rope_heads:
        cos, sin = rope
        per = cos.shape[0] // tm
        in_specs += [pl.BlockSpec((tm, HEAD_DIM), lambda i: (i % per, 0)),
                     pl.BlockSpec((tm, HEAD_DIM), lambda i: (i % per, 0))]
        args += [cos, sin]
    outs = pl.pallas_call(
        functools.partial(_proj_kernel, rope_heads=rope_heads, out_cols=tuple(out_cols), out_scale=out_scale),
        grid=(n // tm,),
        in_specs=in_specs,
        out_specs=[pl.BlockSpec((tm, hi - lo), lambda i: (i, 0)) for _, lo, hi in out_cols],
        out_shape=[jax.ShapeDtypeStruct((n, hi - lo), dt) for dt, lo, hi in out_cols],
        compiler_params=_cparams("parallel"),
        name="proj",
    )(*args)
    return outs


def _out_ln_kernel(a_ref, w_ref, x_ref, gate_ref, g_ref, b_ref, o_ref):
    y = jnp.dot(a_ref[...], w_ref[...], preferred_element_type=F32)
    z = DN_ALPHA * x_ref[...] + (1.0 + gate_ref[0]) * y
    o_ref[...] = _layernorm_rows(z, g_ref[...], b_ref[...])


def _out_ln(a_bf16, w_bf16, x, gate, ln_g, ln_b, seq_len, tm_want=512):
    n, d = x.shape
    k = a_bf16.shape[1]
    tm = _row_tile(n, tm_want)
    gt, gt_blk, gt_idx = _mod_operand(gate, seq_len, tm)
    return pl.pallas_call(
        _out_ln_kernel,
        grid=(n // tm,),
        in_specs=[pl.BlockSpec((tm, k), lambda i: (i, 0)),
                  pl.BlockSpec((k, d), lambda i: (0, 0)),
                  pl.BlockSpec((tm, d), lambda i: (i, 0)),
                  pl.BlockSpec(gt_blk, gt_idx),
                  pl.BlockSpec((1, d), lambda i: (0, 0)),
                  pl.BlockSpec((1, d), lambda i: (0, 0))],
        out_specs=pl.BlockSpec((tm, d), lambda i: (i, 0)),
        out_shape=jax.ShapeDtypeStruct((n, d), F32),
        compiler_params=_cparams("parallel"),
        name="out_ln",
    )(a_bf16, w_bf16, x, gt, ln_g.reshape(1, d), ln_b.reshape(1, d))


def _log_sigmoid(z):
    return jnp.minimum(z, 0.0) - jnp.log(1.0 + jnp.exp(-jnp.abs(z)))


def _gla_kernel(q_ref, k_ref, v_ref, r_ref, g1_ref, wg2_ref, bg_ref, ng_ref, s0_ref,
                og_ref, sfin_ref, s_scr, *, chunk, sub, seq_len, padded_len):
    c = pl.program_id(1)
    n_chunks = padded_len // chunk

    @pl.when(c == 0)
    def _():
        s_scr[...] = s0_ref[0]

    z = jnp.dot(g1_ref[0], wg2_ref[...], precision=lax.Precision.HIGHEST,
                preferred_element_type=F32) + bg_ref[...]
    la = _log_sigmoid(z) * (1.0 / GLA_TAU)
    live = None
    if padded_len != seq_len:
        row = c * chunk + lax.broadcasted_iota(jnp.int32, (chunk, 1), 0)
        live = row < seq_len
        la = jnp.where(live, la, 0.0)
    n_live = min(chunk, seq_len) if n_chunks == 1 else chunk

    ri = lax.broadcasted_iota(jnp.int32, (chunk, chunk), 0)
    ci = lax.broadcasted_iota(jnp.int32, (chunk, chunk), 1)
    tri = (ri >= ci).astype(F32)
    bc_all = jnp.dot(tri, la, precision=lax.Precision.HIGHEST, preferred_element_type=F32)

    for h in range(GLA_HEADS):
        kcols = slice(h * GLA_DK, (h + 1) * GLA_DK)
        vcols = slice(h * GLA_DV, (h + 1) * GLA_DV)
        _gla_head(q_ref[0, :, kcols] * (GLA_DK ** -0.5), k_ref[0, :, kcols], v_ref[0, :, vcols],
                  r_ref[0, :, vcols], bc_all[:, kcols], live, ng_ref, og_ref, vcols, s_scr, h,
                  chunk=chunk, sub=sub, n_live=n_live)

    @pl.when(c == n_chunks - 1)
    def _():
        sfin_ref[0] = s_scr[...]


def _gla_head(q, k, v, r, bc, live, ng_ref, og_ref, vcols, s_scr, h, *, chunk, sub, n_live):
    if live is not None:
        k = jnp.where(live, k, 0.0)
        v = jnp.where(live, v, 0.0)
    s = s_scr[h]
    o = jnp.dot((q * jnp.exp(bc)).astype(BF16), s.astype(BF16), preferred_element_type=F32)
    sub_row = lax.broadcasted_iota(jnp.int32, (sub, 1), 0)
    sub_col = lax.broadcasted_iota(jnp.int32, (8, sub), 1)
    o_parts = []
    for blk in range(chunk // sub):
        lo = blk * sub
        b_i = bc[lo:lo + sub]
        q_i = q[lo:lo + sub]
        o_i = o[lo:lo + sub]
        if blk > 0:
            b_ref = bc[lo - 1:lo]
            q_t = q_i * jnp.exp(b_i - b_ref)
            k_t = k[:lo] * jnp.exp(b_ref - bc[:lo])
            att = lax.dot_general(q_t.astype(BF16), k_t.astype(BF16), (((1,), (1,)), ((), ())),
                                  preferred_element_type=F32)
            o_i = o_i + jnp.dot(att.astype(BF16), v[:lo].astype(BF16), preferred_element_type=F32)
        groups = [(a, min(a + 8, sub)) for a in range(0, sub, 8)]
        att_g = [jnp.zeros((b - a, sub), F32) for a, b in groups]
        for j in range(min(sub, max(n_live - lo, 0))):
            b_j = bc[lo + j:lo + j + 1]
            k_j = k[lo + j:lo + j + 1]
            for gi, (a, b) in enumerate(groups):
                if b <= j:
                    continue
                d = b_i[a:b] - b_j
                if a < j:
                    d = jnp.where(sub_row[a:b] >= j, d, NEG_INF)
                w_col = jnp.sum(q_i[a:b] * k_j * jnp.exp(d), axis=-1, keepdims=True)
                att_g[gi] = jnp.where(sub_col[:b - a] == j, w_col, att_g[gi])
        att_own = jnp.concatenate(att_g, axis=0) if len(att_g) > 1 else att_g[0]
        o_parts.append(o_i + jnp.dot(att_own.astype(BF16), v[lo:lo + sub].astype(BF16),
                                     preferred_element_type=F32))
    o = jnp.concatenate(o_parts, axis=0) if len(o_parts) > 1 else o_parts[0]

    o = o * lax.rsqrt(jnp.mean(o * o, axis=-1, keepdims=True) + LN_EPS) * ng_ref[...]
    og_ref[0, :, vcols] = (o * (r * jax.nn.sigmoid(r))).astype(og_ref.dtype)

    b_last = bc[chunk - 1:chunk]
    k_d = k * jnp.exp(b_last - bc)
    decay = jnp.transpose(jnp.broadcast_to(jnp.exp(b_last), (GLA_DK, GLA_DK)))
    kv = lax.dot_general(k_d.astype(BF16), v.astype(BF16), (((0,), (0,)), ((), ())),
                         preferred_element_type=F32)
    s_scr[h] = jnp.concatenate([decay] * (GLA_DV // GLA_DK), axis=1) * s + kv


def _gla_scan(proj, s0, wg2_pad, b_g, norm_g, batch, seq_len):
    cols = proj.shape[1]
    if seq_len >= 64:
        chunk, sub = 64, 16
    else:
        chunk = sub = 16
    padded = -(-seq_len // chunk) * chunk
    p3 = proj.reshape(batch, seq_len, cols)
    if padded != seq_len:
        p3 = jnp.pad(p3, ((0, 0), (0, padded - seq_len), (0, 0)))
    n_chunks = padded // chunk
    dq = GLA_HEADS * GLA_DK
    dv = GLA_HEADS * GLA_DV
    k0, v0, r0, g0 = 1, 2 * dq // dv, (2 * dq + dv) // dv, (2 * dq + 2 * dv) // LANES
    state_spec = pl.BlockSpec((1, GLA_HEADS, GLA_DK, GLA_DV), lambda b, c: (b, 0, 0, 0))
    og, s_fin = pl.pallas_call(
        functools.partial(_gla_kernel, chunk=chunk, sub=sub, seq_len=seq_len, padded_len=padded),
        grid=(batch, n_chunks),
        in_specs=[pl.BlockSpec((1, chunk, dq), lambda b, c: (b, c, 0)),
                  pl.BlockSpec((1, chunk, dq), lambda b, c: (b, c, k0)),
                  pl.BlockSpec((1, chunk, dv), lambda b, c: (b, c, v0)),
                  pl.BlockSpec((1, chunk, dv), lambda b, c: (b, c, r0)),
                  pl.BlockSpec((1, chunk, LANES), lambda b, c: (b, c, g0)),
                  pl.BlockSpec((LANES, dq), lambda b, c: (0, 0)),
                  pl.BlockSpec((1, dq), lambda b, c: (0, 0)),
                  pl.BlockSpec((1, GLA_DV), lambda b, c: (0, 0)),
                  state_spec],
        out_specs=[pl.BlockSpec((1, chunk, dv), lambda b, c: (b, c, 0)), state_spec],
        out_shape=[jax.ShapeDtypeStruct((batch, padded, dv), BF16),
                   jax.ShapeDtypeStruct((batch, GLA_HEADS, GLA_DK, GLA_DV), F32)],
        scratch_shapes=[pltpu.VMEM((GLA_HEADS, GLA_DK, GLA_DV), F32)],
        compiler_params=_cparams("parallel", "arbitrary"),
        name="gla_scan",
    )(p3, p3, p3, p3, p3, wg2_pad, b_g.reshape(1, dq), norm_g.reshape(1, GLA_DV), s0)
    og = og[:, :seq_len].reshape(batch * seq_len, GLA_HEADS * GLA_DV)
    return og, s_fin


def _gla_layer(x, mods, s0, w_in_pad, wg2_pad, b_g, norm_g, w_o, ln_g, ln_b, batch, seq_len):
    shift, scale, gate = mods
    (proj,) = _proj(x, scale, shift, w_in_pad, seq_len, [(F32, 0, w_in_pad.shape[1])])
    og, s_fin = _gla_scan(proj, s0, wg2_pad, b_g, norm_g, batch, seq_len)
    return _out_ln(og, w_o, x, gate, ln_g, ln_b, seq_len), s_fin


def _top_values(s, count, want_rank=False):
    vals = []
    rank = jnp.full(s.shape, float(count), F32) if want_rank else None
    for r in range(count):
        m = jnp.max(s, axis=0, keepdims=True)
        vals.append(m)
        hit = s == m
        if want_rank:
            rank = jnp.where(hit, float(r), rank)
        if r + 1 < count:
            s = jnp.where(hit, NEG_INF, s)
    return (vals, rank) if want_rank else vals


def _peer_select(q_scr, keys_ref, c_scr, e1_scr, r2_scr, e2_scr, head):
    row0 = pl.multiple_of(head * PEER_DKEY, PEER_DKEY)
    half = PEER_DKEY // 2
    s1_all = jnp.dot(keys_ref[0], q_scr[pl.ds(row0, half), :],
                     precision=lax.Precision.HIGHEST, preferred_element_type=F32)
    s2_all = jnp.dot(keys_ref[1], q_scr[pl.ds(row0 + half, half), :],
                     precision=lax.Precision.HIGHEST, preferred_element_type=F32)
    width = min(LANES, s1_all.shape[1])
    for g in range(s1_all.shape[1] // width):
        cols = slice(g * width, (g + 1) * width)
        _peer_select_group(s1_all[:, cols], s2_all[:, cols], c_scr, e1_scr, r2_scr, e2_scr, head, cols)


def _peer_select_group(s1, s2, c_scr, e1_scr, r2_scr, e2_scr, head, cols):
    a = _top_values(s1, PEER_TOPK)
    b, rank2 = _top_values(s2, PEER_TOPK, want_rank=True)
    s1 = s1 - a[0]
    s2 = s2 - b[0]
    a = [v - a[0] for v in a]
    b = [v - b[0] for v in b]
    cand = [a[i] + b[j] for i in range(PEER_TOPK) for j in range(PEER_TOPK) if (i + 1) * (j + 1) <= PEER_TOPK]
    pad = (-len(cand)) % 8
    cand = jnp.concatenate(cand + [jnp.full_like(cand[0], NEG_INF)] * pad, axis=0)
    top = _top_values(cand, PEER_TOPK)
    cut = top[-1]
    z = jnp.zeros_like(cut)
    for v in top:
        z = z + jnp.exp(v)
    count = jnp.zeros_like(s1)
    for bj in b:
        count = count + jnp.where(s1 + bj >= cut, 1.0, 0.0)
    c_scr[head, :, cols] = count
    e1_scr[head, :, cols] = jnp.exp(s1) / z
    r2_scr[head, :, cols] = rank2.astype(BF16)
    e2_scr[head, :, cols] = jnp.exp(s2).astype(BF16)


def _peer_kernel(x_ref, sc_ref, sh_ref, gate_ref, wq_ref, keys_ref, u_ref, vt_ref, lng_ref, lnb_ref,
                 o_ref, ht_scr, q_scr, c_scr, e1_scr, r2_scr, e2_scr, acc_scr, a_scr, p_scr, *, te, eb):
    j = pl.program_id(1)

    @pl.when(j == 0)
    def _():
        h = x_ref[...] * (1.0 + sc_ref[0]) + sh_ref[0]
        ht_scr[...] = jnp.transpose(h).astype(BF16)
        q_scr[...] = jnp.dot(wq_ref[...], ht_scr[...], preferred_element_type=F32)

        def head_body(head, carry):
            _peer_select(q_scr, keys_ref, c_scr, e1_scr, r2_scr, e2_scr, head)
            return carry
        lax.fori_loop(0, PEER_HEADS, head_body, 0, unroll=2)
        acc_scr[...] = jnp.zeros_like(acc_scr)

    def activations(e):
        a_scr[e % 2] = jnp.dot(u_ref[e * eb:(e + 1) * eb, :], ht_scr[...],
                               preferred_element_type=F32)

    def weigh(e):
        key1 = (j * te + e * eb) // PEER_NKEYS
        for i1 in range(eb // PEER_NKEYS):
            rows = slice(i1 * PEER_NKEYS, (i1 + 1) * PEER_NKEYS)
            w = None
            for head in range(PEER_HEADS):
                row = pl.ds(key1 + i1, 1)
                keep = r2_scr[head] < c_scr[head, row, :].astype(BF16)
                val = jnp.where(keep, e2_scr[head] * e1_scr[head, row, :].astype(BF16), jnp.zeros((), BF16))
                w = val if w is None else w + val
            a_half = a_scr[e % 2, rows, :]
            gelu = a_half * (1.0 + lax.erf(a_half * (2.0 ** 0.5)))
            p_scr[e % 2, rows, :] = w * gelu.astype(BF16)

    def combine(e):
        acc_scr[...] += jnp.dot(vt_ref[:, e * eb:(e + 1) * eb], p_scr[e % 2],
                                preferred_element_type=F32)

    n_blocks = te // eb
    activations(0)
    for e in range(n_blocks):
        if e + 1 < n_blocks:
            activations(e + 1)
        weigh(e)
        combine(e)

    @pl.when(j == pl.num_programs(1) - 1)
    def _():
        out = jnp.transpose(acc_scr[...])
        z = DN_ALPHA * x_ref[...] + (1.0 + gate_ref[0]) * out
        o_ref[...] = _layernorm_rows(z, lng_ref[...], lnb_ref[...])


def _peer_layer(x, mods, wq_t, keys, u_half, v_t, ln_g, ln_b, seq_len, tm_want=512, te=2048, eb=512):
    shift, scale, gate = mods
    n, d = x.shape
    tm = _row_tile(n, tm_want)
    n_exp = u_half.shape[0]
    sc, sc_blk, sc_idx = _mod_operand(scale, seq_len, tm)
    sh, sh_blk, sh_idx = _mod_operand(shift, seq_len, tm)
    gt, gt_blk, gt_idx = _mod_operand(gate, seq_len, tm)
    hq = PEER_HEADS * PEER_DKEY
    row = lambda f: (lambda i, j: f(i))
    return pl.pallas_call(
        functools.partial(_peer_kernel, te=te, eb=eb),
        grid=(n // tm, n_exp // te),
        in_specs=[pl.BlockSpec((tm, d), lambda i, j: (i, 0)),
                  pl.BlockSpec(sc_blk, row(sc_idx)),
                  pl.BlockSpec(sh_blk, row(sh_idx)),
                  pl.BlockSpec(gt_blk, row(gt_idx)),
                  pl.BlockSpec((hq, d), lambda i, j: (0, 0)),
                  pl.BlockSpec((2, PEER_NKEYS, PEER_DKEY // 2), lambda i, j: (0, 0, 0)),
                  pl.BlockSpec((te, d), lambda i, j: (j, 0)),
                  pl.BlockSpec((d, te), lambda i, j: (0, j)),
                  pl.BlockSpec((1, d), lambda i, j: (0, 0)),
                  pl.BlockSpec((1, d), lambda i, j: (0, 0))],
        out_specs=pl.BlockSpec((tm, d), lambda i, j: (i, 0)),
        out_shape=jax.ShapeDtypeStruct((n, d), F32),
        scratch_shapes=[pltpu.VMEM((d, tm), BF16),
                        pltpu.VMEM((hq, tm), F32),
                        pltpu.VMEM((PEER_HEADS, PEER_NKEYS, tm), F32),
                        pltpu.VMEM((PEER_HEADS, PEER_NKEYS, tm), F32),
                        pltpu.VMEM((PEER_HEADS, PEER_NKEYS, tm), BF16),
                        pltpu.VMEM((PEER_HEADS, PEER_NKEYS, tm), BF16),
                        pltpu.VMEM((d, tm), F32),
                        pltpu.VMEM((2, eb, tm), F32),
                        pltpu.VMEM((2, eb, tm), BF16)],
        compiler_params=_cparams("parallel", "arbitrary"),
        name="peer",
    )(x, sc, sh, gt, wq_t, keys, u_half, v_t, ln_g.reshape(1, d), ln_b.reshape(1, d))


def _block_mean_kernel(*refs, n_parts):
    o_ref = refs[-1]
    total = None
    rows = 0
    for p_ref in refs[-1 - n_parts:-1]:
        part = jnp.sum(p_ref[0].astype(F32), axis=0, keepdims=True)
        total = part if total is None else total + part
        rows += p_ref.shape[1]
    o_ref[0] = total * (1.0 / rows)


def _block_means_dense(k, batch, seq_len):
    w = k.shape[1]
    nb = seq_len // MOBA_BLOCK
    out = pl.pallas_call(
        functools.partial(_block_mean_kernel, n_parts=1),
        grid=(batch * nb,),
        in_specs=[pl.BlockSpec((1, MOBA_BLOCK, w), lambda i: (i, 0, 0))],
        out_specs=pl.BlockSpec((1, 1, w), lambda i: (i, 0, 0)),
        out_shape=jax.ShapeDtypeStruct((batch * nb, 1, w), F32),
        compiler_params=_cparams("parallel"),
        name="block_means",
    )(k.reshape(batch * nb, MOBA_BLOCK, w))
    return out.reshape(batch, nb, w)


def _moba_prompt_kernel(qi_ref, kv0_ref, kv1_ref, q_ref, k0_ref, v0_ref, k1_ref, v1_ref, mean_ref, o_ref,
                        m_scr, acc_scr, qa_scr):
    t = pl.program_id(1)
    qi = qi_ref[t]
    kv0 = kv0_ref[t]
    kv1 = kv1_ref[t]
    own = kv0 == qi
    blk = MOBA_BLOCK
    masked = 1e30
    row = lax.broadcasted_iota(jnp.int32, (blk, blk), 0)
    col = lax.broadcasted_iota(jnp.int32, (blk, blk), 1)
    causal_bias = jnp.where(own, jnp.where(col <= row, 0.0, NEG_INF), 0.0)
    gcol_i = lax.broadcasted_iota(jnp.int32, (blk, LANES), 1)
    head_cols = [slice(h * HEAD_DIM, (h + 1) * HEAD_DIM) for h in range(MOBA_HEADS)]

    @pl.when(own)
    def _():
        n_rows = mean_ref.shape[1]
        grow = lax.broadcasted_iota(jnp.int32, (n_rows, blk), 0).astype(F32)
        n_past = qi.astype(F32)
        for h, cs in enumerate(head_cols):
            g = lax.dot_general(mean_ref[0, :, cs], q_ref[0, :, cs].astype(F32), (((1,), (1,)), ((), ())),
                                precision=lax.Precision.HIGHEST, preferred_element_type=F32)
            g = jnp.where(grow < n_past, g, NEG_INF)
            bias_t = jnp.where(grow == n_past, 0.0, -masked)
            for r in range(MOBA_TOPK):
                best = jnp.max(g, axis=0, keepdims=True)
                pick = jnp.min(jnp.where(g == best, grow, float(n_rows)), axis=0, keepdims=True)
                hit = grow == pick
                bias_t = jnp.where(r < qi, jnp.where(hit, 0.0, bias_t), bias_t)
                g = jnp.where(hit, NEG_INF, g)
            if n_rows < LANES:
                bias_t = jnp.concatenate([bias_t, jnp.full((LANES - n_rows, blk), -masked, F32)], axis=0)
            qa_scr[h] = jnp.concatenate([q_ref[0, :, cs], jnp.transpose(bias_t).astype(BF16)], axis=1)
            m_scr[h] = jnp.full((blk, LANES), NEG_INF, F32)
            acc_scr[h] = jnp.zeros((blk, 2 * HEAD_DIM), F32)

    onehot0 = jnp.where(gcol_i == kv0, 1.0, 0.0).astype(BF16)
    onehot1 = jnp.where(gcol_i == kv1, 1.0, 0.0).astype(BF16)
    absent1 = jnp.where(kv1 < 0, -masked, 0.0)
    ones = jnp.ones((blk, HEAD_DIM), BF16)
    twice = lambda a: jnp.concatenate([a, a], axis=1)
    nt = (((1,), (1,)), ((), ()))
    for h, cs in enumerate(head_cols):
        qa = qa_scr[h]
        s0 = lax.dot_general(qa, jnp.concatenate([k0_ref[0, :, cs], onehot0], axis=1), nt,
                             preferred_element_type=F32) + causal_bias
        s1 = lax.dot_general(qa, jnp.concatenate([k1_ref[0, :, cs], onehot1], axis=1), nt,
                             preferred_element_type=F32) + absent1
        m_old = m_scr[h]
        m_new = jnp.maximum(m_old, jnp.maximum(jnp.max(s0, axis=-1, keepdims=True),
                                               jnp.max(s1, axis=-1, keepdims=True)))
        alpha = jnp.exp(m_old - m_new)
        p0 = jnp.exp(s0 - twice(m_new)).astype(BF16)
        p1 = jnp.exp(s1 - twice(m_new)).astype(BF16)
        acc_scr[h] = (twice(alpha) * acc_scr[h]
                      + jnp.dot(p0, jnp.concatenate([v0_ref[0, :, cs], ones], axis=1), preferred_element_type=F32)
                      + jnp.dot(p1, jnp.concatenate([v1_ref[0, :, cs], ones], axis=1), preferred_element_type=F32))
        m_scr[h] = m_new

    @pl.when(kv0 <= 1)
    def _():
        for h, cs in enumerate(head_cols):
            acc = acc_scr[h]
            o_ref[0, :, cs] = (acc[:, :HEAD_DIM] / acc[:, HEAD_DIM:]).astype(o_ref.dtype)


def _moba_prompt(q, k, v, means, batch, seq_len):
    w = q.shape[1]
    nb = seq_len // MOBA_BLOCK
    assert nb <= LANES
    n_rows = -(-nb // 8) * 8
    means = jnp.pad(means, ((0, 0), (0, n_rows - nb), (0, 0)))
    steps = [(qi, kv, kv - 1) for qi in range(nb) for kv in range(qi, -1, -2)]
    qi_tab, kv0_tab, kv1_tab = (jnp.asarray(col, jnp.int32) for col in zip(*steps))
    r3 = lambda a: a.reshape(batch, seq_len, w)
    q_spec = pl.BlockSpec((1, MOBA_BLOCK, w), lambda b, t, qt, k0, k1: (b, qt[t], 0))
    kv0_spec = pl.BlockSpec((1, MOBA_BLOCK, w), lambda b, t, qt, k0, k1: (b, k0[t], 0))
    kv1_spec = pl.BlockSpec((1, MOBA_BLOCK, w), lambda b, t, qt, k0, k1: (b, jnp.maximum(k1[t], 0), 0))
    out = pl.pallas_call(
        _moba_prompt_kernel,
        grid_spec=pltpu.PrefetchScalarGridSpec(
            num_scalar_prefetch=3,
            grid=(batch, len(steps)),
            in_specs=[q_spec, kv0_spec, kv0_spec, kv1_spec, kv1_spec,
                      pl.BlockSpec((1, n_rows, w), lambda b, t, qt, k0, k1: (b, 0, 0))],
            out_specs=q_spec,
            scratch_shapes=[pltpu.VMEM((MOBA_HEADS, MOBA_BLOCK, LANES), F32),
                            pltpu.VMEM((MOBA_HEADS, MOBA_BLOCK, 2 * HEAD_DIM), F32),
                            pltpu.VMEM((MOBA_HEADS, MOBA_BLOCK, HEAD_DIM + LANES), BF16)]),
        out_shape=jax.ShapeDtypeStruct((batch, seq_len, w), BF16),
        compiler_params=_cparams("parallel", "arbitrary"),
        name="moba_prompt",
    )(qi_tab, kv0_tab, kv1_tab, r3(q), r3(k), r3(v), r3(k), r3(v), means)
    return out.reshape(batch * seq_len, w)


def _moba_decode_kernel(pt_ref, q_ref, kn_ref, vn_ref, *rest, n_new, n_past, ppb):
    k_pages = rest[:ppb]
    v_pages = rest[ppb:2 * ppb]
    o_ref, g_scr, m_scr, l_scr, acc_scr = rest[2 * ppb:]
    j = pl.program_id(1)
    nq = MOBA_HEADS * n_new
    head_bits = MOBA_HEADS.bit_length() - 1
    new_bits = n_new.bit_length() - 1
    q = q_ref[0]
    qb = q.astype(BF16)

    def same_head(n_cols):
        lane_head = lax.broadcasted_iota(jnp.int32, (nq, n_cols), 1) & (MOBA_HEADS - 1)
        row_head = lax.broadcasted_iota(jnp.int32, (nq, n_cols), 0) >> new_bits
        return lane_head == row_head

    def scores(keys):
        return lax.dot_general(qb, keys, (((1,), (1,)), ((), ())), preferred_element_type=F32)

    page_len = k_pages[0].shape[1]
    page_rows = page_len * MOBA_HEADS
    pages_per_block = MOBA_BLOCK // page_len
    head_bias = jnp.where(same_head(page_rows), 0.0, NEG_INF)
    for bi in range(ppb // pages_per_block):
        block = j * (ppb // pages_per_block) + bi
        k_blk = k_pages[bi * pages_per_block:(bi + 1) * pages_per_block]
        v_blk = v_pages[bi * pages_per_block:(bi + 1) * pages_per_block]
        key_sum = None
        for k_ref in k_blk:
            part = jnp.sum(k_ref[0], axis=0)
            key_sum = part if key_sum is None else key_sum + part
        gate = lax.dot_general(q, key_sum * (1.0 / MOBA_BLOCK), (((1,), (1,)), ((), ())),
                               precision=lax.Precision.HIGHEST, preferred_element_type=F32)
        g_scr[block] = jnp.sum(jnp.where(same_head(MOBA_HEADS), gate, 0.0), axis=-1, keepdims=True)
        s_pages = [scores(k_ref[0].reshape(page_rows, HEAD_DIM).astype(BF16)) + head_bias for k_ref in k_blk]
        m = None
        for s in s_pages:
            s_max = jnp.max(s, axis=-1, keepdims=True)
            m = s_max if m is None else jnp.maximum(m, s_max)
        l = None
        acc = None
        for s, v_ref in zip(s_pages, v_blk):
            p = jnp.exp(s - m)
            p_sum = jnp.sum(p, axis=-1, keepdims=True)
            pv = jnp.dot(p.astype(BF16), v_ref[0].reshape(page_rows, HEAD_DIM).astype(BF16),
                         preferred_element_type=F32)
            l = p_sum if l is None else l + p_sum
            acc = pv if acc is None else acc + pv
        m_scr[block] = m
        l_scr[block] = l
        acc_scr[block] = acc

    @pl.when(j == pl.num_programs(1) - 1)
    def _():
        s = scores(kn_ref[0].astype(BF16))
        key_t = lax.broadcasted_iota(jnp.int32, s.shape, 1) >> head_bits
        qry_r = lax.broadcasted_iota(jnp.int32, s.shape, 0) & (n_new - 1)
        s = jnp.where(same_head(s.shape[1]), jnp.where(key_t <= qry_r, s, NEG_INF), NEG_INF)
        m_own = jnp.max(s, axis=-1, keepdims=True)
        p = jnp.exp(s - m_own)
        l_own = jnp.sum(p, axis=-1, keepdims=True)
        acc_own = jnp.dot(p.astype(BF16), vn_ref[0].astype(BF16), preferred_element_type=F32)

        k_sel = min(MOBA_TOPK, n_past)
        gates = [g_scr[b] for b in range(n_past)]
        weights = []
        m_all = m_own
        for b in range(n_past):
            place = jnp.zeros_like(gates[b])
            for o in range(n_past):
                if o != b:
                    ahead = (gates[o] >= gates[b]) if o < b else (gates[o] > gates[b])
                    place = place + jnp.where(ahead, 1.0, 0.0)
            weights.append(jnp.where(place < k_sel, 0.0, NEG_INF))
            m_all = jnp.maximum(m_all, m_scr[b] + weights[b])
        w_own = jnp.exp(m_own - m_all)
        l_all = w_own * l_own
        acc_all = w_own * acc_own
        for b in range(n_past):
            w_b = jnp.exp(m_scr[b] + weights[b] - m_all)
            l_all = l_all + w_b * l_scr[b]
            acc_all = acc_all + w_b * acc_scr[b]
        o_ref[0] = acc_all / l_all


def _moba_decode(q, k_new, v_new, cache_k, cache_v, page_table_flat, batch, n_new, pages_per_seq):
    _, page, heads, hd = cache_k.shape
    assert heads & (heads - 1) == 0 and n_new & (n_new - 1) == 0
    n_past = pages_per_seq * page // MOBA_BLOCK
    blocks_per_step = next(c for c in (4, 2, 1) if n_past % c == 0)
    ppb = (MOBA_BLOCK // page) * blocks_per_step
    nq = heads * n_new
    q_hq = q.reshape(batch, n_new, heads, hd).transpose(0, 2, 1, 3).reshape(batch, nq, hd)
    kv_rows = lambda a: a.reshape(batch, nq, hd)
    seq_spec = pl.BlockSpec((1, nq, hd), lambda b, j, pt: (b, 0, 0))
    page_spec = lambda part: pl.BlockSpec(
        (1, page, heads, hd), lambda b, j, pt: (pt[b * pages_per_seq + j * ppb + part], 0, 0, 0))
    out = pl.pallas_call(
        functools.partial(_moba_decode_kernel, n_new=n_new, n_past=n_past, ppb=ppb),
        grid_spec=pltpu.PrefetchScalarGridSpec(
            num_scalar_prefetch=1,
            grid=(batch, n_past // blocks_per_step),
            in_specs=[seq_spec, seq_spec, seq_spec] + [page_spec(p) for p in range(ppb)] * 2,
            out_specs=seq_spec,
            scratch_shapes=[pltpu.VMEM((n_past, nq, 1), F32),
                            pltpu.VMEM((n_past, nq, 1), F32),
                            pltpu.VMEM((n_past, nq, 1), F32),
                            pltpu.VMEM((n_past, nq, hd), F32)]),
        out_shape=jax.ShapeDtypeStruct((batch, nq, hd), F32),
        compiler_params=_cparams("parallel", "arbitrary"),
        name="moba_decode",
    )(page_table_flat, q_hq, kv_rows(k_new), kv_rows(v_new),
      *([cache_k] * ppb), *([cache_v] * ppb))
    return out.reshape(batch, heads, n_new, hd).transpose(0, 2, 1, 3).reshape(batch * n_new, heads * hd)


def _rope_tables(pos):
    half = HEAD_DIM // 2
    inv = ROPE_THETA ** (-jnp.arange(half, dtype=F32) / half)
    ang = pos.astype(F32)[:, None] * inv[None, :]
    cos, sin = jnp.cos(ang), jnp.sin(ang)
    return jnp.concatenate([cos, cos], axis=-1), jnp.concatenate([-sin, sin], axis=-1)


def _trunk(x, mods, mods_kv, s0, past, p, batch, seq_len):
    pos0 = 0 if past is None else past[3] * past[0].shape[1]
    pos = pos0 + jnp.arange(seq_len, dtype=jnp.int32)
    cos, sin = _rope_tables(pos)
    n = batch * seq_len
    if seq_len % _row_tile(n, 512):
        cos, sin = jnp.tile(cos, (batch, 1)), jnp.tile(sin, (batch, 1))
    gla_fin = []
    k_new = v_new = None
    dm = MOBA_HEADS * HEAD_DIM
    for layer in range(DEPTH):
        shift, scale, gate = mods[layer][0]
        if layer < N_A:
            x, s_fin = _gla_layer(x, mods[layer][0], s0[layer], p["gla_w_in"][layer], p["gla_w_g2"][layer],
                                  p["gla_b_g"][layer], p["gla_norm_g"][layer], p["gla_w_o"][layer],
                                  p["ln_g"][layer, 0], p["ln_b"][layer, 0], batch, seq_len)
            gla_fin.append(s_fin)
        else:
            jb = layer - N_A
            (q,) = _proj(x, scale, shift, p["moba_w_q"][jb], seq_len, [(BF16 if past is None else F32, 0, dm)],
                         rope=(cos, sin), rope_heads=MOBA_HEADS, out_scale=HEAD_DIM ** -0.5)
            if past is None:
                o = _moba_prompt(q, k_bf16, v_bf16, means, batch, seq_len)
            else:
                o = _moba_decode(q, k_new, v_new, past[0], past[1], past[2], batch, seq_len,
                                 past[3]).astype(BF16)
            x = _out_ln(o, p["moba_w_o"][jb], x, gate, p["ln_g"][layer, 0], p["ln_b"][layer, 0], seq_len)
        x = _peer_layer(x, mods[layer][1], p["peer_w_q"][layer], p["peer_keys"][layer], p["peer_u"][layer],
                        p["peer_v"][layer], p["ln_g"][layer, 1], p["ln_b"][layer, 1], seq_len)
        if layer == N_A - 1:
            kv_shift, kv_scale = mods_kv
            kv_cols = [(F32, 0, dm), (F32, dm, 2 * dm)]
            if past is None:
                kv_cols += [(BF16, 0, dm), (BF16, dm, 2 * dm)]
            kv = _proj(x, kv_scale, kv_shift, p["w_kv"], seq_len, kv_cols, rope=(cos, sin),
                       rope_heads=MOBA_HEADS)
            k_new, v_new = kv[:2]
            if past is None:
                k_bf16, v_bf16 = kv[2:]
                means = _block_means_dense(k_new, batch, seq_len)
    return x, jnp.stack(gla_fin), k_new, v_new


def kernel(x_prompt, x_sample, c_prompt, c_sample, state_gla, cache_k, cache_v, page_table,
           w_ada, b_ada, ln_g, ln_b, gla_w_in, gla_w_g2, gla_b_g, gla_norm_g, gla_w_o,
           w_ada_kv, b_ada_kv, w_kv, moba_w_q, moba_w_o, peer_w_q, peer_keys, peer_u, peer_v):
    b_p, l_p, d = x_prompt.shape
    b_s, l_s, _ = x_sample.shape
    gla_cols = -(-gla_w_in.shape[-1] // LANES) * LANES
    p = {
        "ln_g": ln_g, "ln_b": ln_b,
        "gla_w_in": jnp.pad(gla_w_in, ((0, 0), (0, 0), (0, gla_cols - gla_w_in.shape[-1]))).astype(BF16),
        "gla_w_g2": jnp.pad(gla_w_g2, ((0, 0), (0, LANES - GLA_GATE_RANK), (0, 0))),
        "gla_b_g": gla_b_g, "gla_norm_g": gla_norm_g,
        "gla_w_o": gla_w_o.astype(BF16),
        "w_kv": w_kv.astype(BF16),
        "moba_w_q": moba_w_q.astype(BF16), "moba_w_o": moba_w_o.astype(BF16),
        "peer_w_q": jnp.swapaxes(peer_w_q, 1, 2).astype(BF16),
        "peer_keys": peer_keys,
        "peer_u": (0.5 * peer_u).astype(BF16),
        "peer_v": jnp.swapaxes(peer_v, 1, 2).astype(BF16),
    }
    c_all = jnp.concatenate([c_prompt, c_sample], axis=0)
    rows = c_all.shape[0]
    c_all = jnp.pad(c_all, ((0, (-rows) % 8), (0, 0)))
    m_all = _ada_all(c_all, w_ada.reshape(DEPTH * 2, d, 3 * d), b_ada.reshape(DEPTH * 2, 1, 3 * d))
    m_kv = _ada_all(c_all, w_ada_kv[None], b_ada_kv[None, None])[0]

    def group_mods(lo, hi):
        mods = [[tuple(m_all[layer * 2 + sub, lo:hi, i * d:(i + 1) * d] for i in range(3))
                 for sub in range(2)] for layer in range(DEPTH)]
        return mods, (m_kv[lo:hi, :d], m_kv[lo:hi, d:])

    mods_p, kv_p = group_mods(0, b_p)
    mods_s, kv_s = group_mods(b_p, b_p + b_s)

    zero_state = jnp.zeros((N_A, b_p) + state_gla.shape[2:], F32)
    y_p, gla_p, k_p, v_p = _trunk(x_prompt.reshape(b_p * l_p, d), mods_p, kv_p, zero_state, None, p, b_p, l_p)

    past = (cache_k, cache_v, page_table.reshape(-1).astype(jnp.int32), page_table.shape[1])
    y_s, gla_s, k_s, v_s = _trunk(x_sample.reshape(b_s * l_s, d), mods_s, kv_s, state_gla, past, p, b_s, l_s)

    heads = lambda a, b, l: a.reshape(b, l, MOBA_HEADS, HEAD_DIM)
    return (y_p.reshape(b_p, l_p, d), y_s.reshape(b_s, l_s, d), gla_p, gla_s,
            heads(k_p, b_p, l_p), heads(v_p, b_p, l_p), heads(k_s, b_s, l_s), heads(v_s, b_s, l_s))
```

```python
import functools
import math

import jax
import jax.numpy as jnp
from jax import lax
from jax.experimental import pallas as pl
from jax.experimental.pallas import tpu as pltpu

F32 = jnp.float32
BF16 = jnp.bfloat16

D_MODEL = 1024
DEPTH = 4
N_A = DEPTH // 2
N_B = DEPTH - N_A
GLA_HEADS = 4
GLA_DK = D_MODEL // 2 // GLA_HEADS
GLA_DV = D_MODEL // GLA_HEADS
GLA_GATE_RANK = 16
GLA_TAU = 16.0
MOBA_HEADS = 8
HEAD_DIM = D_MODEL // MOBA_HEADS
MOBA_BLOCK = 256
MOBA_TOPK = 3
ROPE_THETA = 10000.0
PEER_HEADS = 8
PEER_NKEYS = 128
PEER_EXPERTS = PEER_NKEYS * PEER_NKEYS
PEER_DKEY = 256
PEER_TOPK = 16
DN_ALPHA = (2.0 * DEPTH) ** 0.25
LN_EPS = 1e-5

LANES = 128
VMEM_LIMIT = 56 * 1024 * 1024
NEG_INF = float("-inf")


def _cparams(*sem):
    return pltpu.CompilerParams(dimension_semantics=sem, vmem_limit_bytes=VMEM_LIMIT)


def _row_tile(n_rows, want):
    t = min(want, n_rows)
    while n_rows % t:
        t //= 2
    return t


def _mod_operand(m, seq_len, tm):
    d = m.shape[-1]
    if seq_len % tm == 0:
        per = seq_len // tm
        return m[:, None, :], (1, 1, d), (lambda i: (i // per, 0, 0))
    rows = jnp.repeat(m, seq_len, axis=0)
    return rows.reshape(rows.shape[0] // tm, tm, d), (1, tm, d), (lambda i: (i, 0, 0))


def _layernorm_rows(z, g, b):
    mu = jnp.mean(z, axis=-1, keepdims=True)
    zc = z - mu
    var = jnp.mean(zc * zc, axis=-1, keepdims=True)
    return zc * lax.rsqrt(var + LN_EPS) * g + b


def _ada_kernel(c_ref, w_ref, b_ref, o_ref):
    c = c_ref[...]
    s = c * jax.nn.sigmoid(c)
    o_ref[0] = jnp.dot(s.astype(BF16), w_ref[0].astype(BF16), preferred_element_type=F32) + b_ref[0]


def _ada_all(c, w, b):
    g, d, n = w.shape
    r = c.shape[0]
    tn = 1024
    return pl.pallas_call(
        _ada_kernel,
        grid=(g, n // tn),
        in_specs=[pl.BlockSpec((r, d), lambda gi, j: (0, 0)),
                  pl.BlockSpec((1, d, tn), lambda gi, j: (gi, 0, j)),
                  pl.BlockSpec((1, 1, tn), lambda gi, j: (gi, 0, j))],
        out_specs=pl.BlockSpec((1, r, tn), lambda gi, j: (gi, 0, j)),
        out_shape=jax.ShapeDtypeStruct((g, r, n), F32),
        compiler_params=_cparams("parallel", "parallel"),
        name="ada",
    )(c, w, b)


def _rope_cols(y, cos, sin, heads):
    outs = []
    for hh in range(heads):
        yh = y[:, hh * HEAD_DIM:(hh + 1) * HEAD_DIM]
        outs.append(yh * cos + pltpu.roll(yh, HEAD_DIM // 2, axis=1) * sin)
    return outs


def _proj_kernel(x_ref, sc_ref, sh_ref, w_ref, *rest, rope_heads, out_cols, out_scale):
    if rope_heads:
        cos_ref, sin_ref = rest[:2]
        outs = rest[2:]
    else:
        outs = rest
    h = x_ref[...] * (1.0 + sc_ref[0]) + sh_ref[0]
    y = jnp.dot(h.astype(BF16), w_ref[...], preferred_element_type=F32)
    if out_scale != 1.0:
        y = y * out_scale
    pieces = []
    if rope_heads:
        parts = _rope_cols(y, cos_ref[...], sin_ref[...], rope_heads)
        pieces += [(hh * HEAD_DIM, p) for hh, p in enumerate(parts)]
    plain = rope_heads * HEAD_DIM
    for lo, hi in sorted({(max(lo, plain), hi) for _, lo, hi in out_cols if hi > plain}):
        pieces.append((lo, y[:, lo:hi]))
    for o, (_, lo, hi) in zip(outs, out_cols):
        for start, vals in pieces:
            if lo <= start and start + vals.shape[1] <= hi:
                o[:, start - lo:start - lo + vals.shape[1]] = vals.astype(o.dtype)


def _proj(x, scale, shift, w_bf16, seq_len, out_cols, rope=None, rope_heads=0, out_scale=1.0, tm_want=512):
    n, d = x.shape
    n_out = w_bf16.shape[1]
    tm = _row_tile(n, tm_want)
    sc, sc_blk, sc_idx = _mod_operand(scale, seq_len, tm)
    sh, sh_blk, sh_idx = _mod_operand(shift, seq_len, tm)
    in_specs = [pl.BlockSpec((tm, d), lambda i: (i, 0)),
                pl.BlockSpec(sc_blk, sc_idx),
                pl.BlockSpec(sh_blk, sh_idx),
                pl.BlockSpec((d, n_out), lambda i: (0, 0))]
    args = [x, sc, sh, w_bf16]
    if rope_heads:
        cos, sin = rope
        per = cos.shape[0] // tm
        in_specs += [pl.BlockSpec((tm, HEAD_DIM), lambda i: (i % per, 0)),
                     pl.BlockSpec((tm, HEAD_DIM), lambda i: (i % per, 0))]
        args += [cos, sin]
    outs = pl.pallas_call(
        functools.partial(_proj_kernel, rope_heads=rope_heads, out_cols=tuple(out_cols), out_scale=out_scale),
        grid=(n // tm,),
        in_specs=in_specs,
        out_specs=[pl.BlockSpec((tm, hi - lo), lambda i: (i, 0)) for _, lo, hi in out_cols],
        out_shape=[jax.ShapeDtypeStruct((n, hi - lo), dt) for dt, lo, hi in out_cols],
        compiler_params=_cparams("parallel"),
        name="proj",
    )(*args)
    return outs


def _out_ln_kernel(a_ref, w_ref, x_ref, gate_ref, g_ref, b_ref, o_ref):
    y = jnp.dot(a_ref[...], w_ref[...], preferred_element_type=F32)
    z = DN_ALPHA * x_ref[...] + (1.0 + gate_ref[0]) * y
    o_ref[...] = _layernorm_rows(z, g_ref[...], b_ref[...])


def _out_ln(a_bf16, w_bf16, x, gate, ln_g, ln_b, seq_len, tm_want=512):
    n, d = x.shape
    k = a_bf16.shape[1]
    tm = _row_tile(n, tm_want)
    gt, gt_blk, gt_idx = _mod_operand(gate, seq_len, tm)
    return pl.pallas_call(
        _out_ln_kernel,
        grid=(n // tm,),
        in_specs=[pl.BlockSpec((tm, k), lambda i: (i, 0)),
                  pl.BlockSpec((k, d), lambda i: (0, 0)),
                  pl.BlockSpec((tm, d), lambda i: (i, 0)),
                  pl.BlockSpec(gt_blk, gt_idx),
                  pl.BlockSpec((1, d), lambda i: (0, 0)),
                  pl.BlockSpec((1, d), lambda i: (0, 0))],
        out_specs=pl.BlockSpec((tm, d), lambda i: (i, 0)),
        out_shape=jax.ShapeDtypeStruct((n, d), F32),
        compiler_params=_cparams("parallel"),
        name="out_ln",
    )(a_bf16, w_bf16, x, gt, ln_g.reshape(1, d), ln_b.reshape(1, d))


def _log_sigmoid(z):
    return jnp.minimum(z, 0.0) - jnp.log(1.0 + jnp.exp(-jnp.abs(z)))


def _gla_kernel(q_ref, k_ref, v_ref, r_ref, g1_ref, wg2_ref, bg_ref, ng_ref, s0_ref,
                og_ref, sfin_ref, s_scr, *, chunk, sub, seq_len, padded_len):
    c = pl.program_id(1)
    n_chunks = padded_len // chunk

    @pl.when(c == 0)
    def _():
        s_scr[...] = s0_ref[0]

    z = jnp.dot(g1_ref[0], wg2_ref[...], precision=lax.Precision.HIGHEST,
                preferred_element_type=F32) + bg_ref[...]
    la = _log_sigmoid(z) * (1.0 / GLA_TAU)
    live = None
    if padded_len != seq_len:
        row = c * chunk + lax.broadcasted_iota(jnp.int32, (chunk, 1), 0)
        live = row < seq_len
        la = jnp.where(live, la, 0.0)
    n_live = min(chunk, seq_len) if n_chunks == 1 else chunk

    ri = lax.broadcasted_iota(jnp.int32, (chunk, chunk), 0)
    ci = lax.broadcasted_iota(jnp.int32, (chunk, chunk), 1)
    tri = (ri >= ci).astype(F32)
    bc_all = jnp.dot(tri, la, precision=lax.Precision.HIGHEST, preferred_element_type=F32)

    for h in range(GLA_HEADS):
        kcols = slice(h * GLA_DK, (h + 1) * GLA_DK)
        vcols = slice(h * GLA_DV, (h + 1) * GLA_DV)
        _gla_head(q_ref[0, :, kcols] * (GLA_DK ** -0.5), k_ref[0, :, kcols], v_ref[0, :, vcols],
                  r_ref[0, :, vcols], bc_all[:, kcols], live, ng_ref, og_ref, vcols, s_scr, h,
                  chunk=chunk, sub=sub, n_live=n_live)

    @pl.when(c == n_chunks - 1)
    def _():
        sfin_ref[0] = s_scr[...]


def _gla_head(q, k, v, r, bc, live, ng_ref, og_ref, vcols, s_scr, h, *, chunk, sub, n_live):
    if live is not None:
        k = jnp.where(live, k, 0.0)
        v = jnp.where(live, v, 0.0)
    s = s_scr[h]
    o = jnp.dot((q * jnp.exp(bc)).astype(BF16), s.astype(BF16), preferred_element_type=F32)
    sub_row = lax.broadcasted_iota(jnp.int32, (sub, 1), 0)
    sub_col = lax.broadcasted_iota(jnp.int32, (8, sub), 1)
    o_parts = []
    for blk in range(chunk // sub):
        lo = blk * sub
        b_i = bc[lo:lo + sub]
        q_i = q[lo:lo + sub]
        o_i = o[lo:lo + sub]
        if blk > 0:
            b_ref = bc[lo - 1:lo]
            q_t = q_i * jnp.exp(b_i - b_ref)
            k_t = k[:lo] * jnp.exp(b_ref - bc[:lo])
            att = lax.dot_general(q_t.astype(BF16), k_t.astype(BF16), (((1,), (1,)), ((), ())),
                                  preferred_element_type=F32)
            o_i = o_i + jnp.dot(att.astype(BF16), v[:lo].astype(BF16), preferred_element_type=F32)
        groups = [(a, min(a + 8, sub)) for a in range(0, sub, 8)]
        att_g = [jnp.zeros((b - a, sub), F32) for a, b in groups]
        for j in range(min(sub, max(n_live - lo, 0))):
            b_j = bc[lo + j:lo + j + 1]
            k_j = k[lo + j:lo + j + 1]
            for gi, (a, b) in enumerate(groups):
                if b <= j:
                    continue
                d = b_i[a:b] - b_j
                if a < j:
                    d = jnp.where(sub_row[a:b] >= j, d, NEG_INF)
                w_col = jnp.sum(q_i[a:b] * k_j * jnp.exp(d), axis=-1, keepdims=True)
                att_g[gi] = jnp.where(sub_col[:b - a] == j, w_col, att_g[gi])
        att_own = jnp.concatenate(att_g, axis=0) if len(att_g) > 1 else att_g[0]
        o_parts.append(o_i + jnp.dot(att_own.astype(BF16), v[lo:lo + sub].astype(BF16),
                                     preferred_element_type=F32))
    o = jnp.concatenate(o_parts, axis=0) if len(o_parts) > 1 else o_parts[0]

    o = o * lax.rsqrt(jnp.mean(o * o, axis=-1, keepdims=True) + LN_EPS) * ng_ref[...]
    og_ref[0, :, vcols] = (o * (r * jax.nn.sigmoid(r))).astype(og_ref.dtype)

    b_last = bc[chunk - 1:chunk]
    k_d = k * jnp.exp(b_last - bc)
    decay = jnp.transpose(jnp.broadcast_to(jnp.exp(b_last), (GLA_DK, GLA_DK)))
    kv = lax.dot_general(k_d.astype(BF16), v.astype(BF16), (((0,), (0,)), ((), ())),
                         preferred_element_type=F32)
    s_scr[h] = jnp.concatenate([decay] * (GLA_DV // GLA_DK), axis=1) * s + kv


def _gla_scan(proj, s0, wg2_pad, b_g, norm_g, batch, seq_len):
    cols = proj.shape[1]
    if seq_len >= 64:
        chunk, sub = 64, 16
    else:
        chunk = sub = 16
    padded = -(-seq_len // chunk) * chunk
    p3 = proj.reshape(batch, seq_len, cols)
    if padded != seq_len:
        p3 = jnp.pad(p3, ((0, 0), (0, padded - seq_len), (0, 0)))
    n_chunks = padded // chunk
    dq = GLA_HEADS * GLA_DK
    dv = GLA_HEADS * GLA_DV
    k0, v0, r0, g0 = 1, 2 * dq // dv, (2 * dq + dv) // dv, (2 * dq + 2 * dv) // LANES
    state_spec = pl.BlockSpec((1, GLA_HEADS, GLA_DK, GLA_DV), lambda b, c: (b, 0, 0, 0))
    og, s_fin = pl.pallas_call(
        functools.partial(_gla_kernel, chunk=chunk, sub=sub, seq_len=seq_len, padded_len=padded),
        grid=(batch, n_chunks),
        in_specs=[pl.BlockSpec((1, chunk, dq), lambda b, c: (b, c, 0)),
                  pl.BlockSpec((1, chunk, dq), lambda b, c: (b, c, k0)),
                  pl.BlockSpec((1, chunk, dv), lambda b, c: (b, c, v0)),
                  pl.BlockSpec((1, chunk, dv), lambda b, c: (b, c, r0)),
                  pl.BlockSpec((1, chunk, LANES), lambda b, c: (b, c, g0)),
                  pl.BlockSpec((LANES, dq), lambda b, c: (0, 0)),
                  pl.BlockSpec((1, dq), lambda b, c: (0, 0)),
                  pl.BlockSpec((1, GLA_DV), lambda b, c: (0, 0)),
                  state_spec],
        out_specs=[pl.BlockSpec((1, chunk, dv), lambda b, c: (b, c, 0)), state_spec],
        out_shape=[jax.ShapeDtypeStruct((batch, padded, dv), BF16),
                   jax.ShapeDtypeStruct((batch, GLA_HEADS, GLA_DK, GLA_DV), F32)],
        scratch_shapes=[pltpu.VMEM((GLA_HEADS, GLA_DK, GLA_DV), F32)],
        compiler_params=_cparams("parallel", "arbitrary"),
        name="gla_scan",
    )(p3, p3, p3, p3, p3, wg2_pad, b_g.reshape(1, dq), norm_g.reshape(1, GLA_DV), s0)
    og = og[:, :seq_len].reshape(batch * seq_len, GLA_HEADS * GLA_DV)
    return og, s_fin


def _gla_layer(x, mods, s0, w_in_pad, wg2_pad, b_g, norm_g, w_o, ln_g, ln_b, batch, seq_len):
    shift, scale, gate = mods
    (proj,) = _proj(x, scale, shift, w_in_pad, seq_len, [(F32, 0, w_in_pad.shape[1])])
    og, s_fin = _gla_scan(proj, s0, wg2_pad, b_g, norm_g, batch, seq_len)
    return _out_ln(og, w_o, x, gate, ln_g, ln_b, seq_len), s_fin


def _top_values(s, count, want_rank=False):
    vals = []
    rank = jnp.full(s.shape, float(count), F32) if want_rank else None
    for r in range(count):
        m = jnp.max(s, axis=0, keepdims=True)
        vals.append(m)
        hit = s == m
        if want_rank:
            rank = jnp.where(hit, float(r), rank)
        if r + 1 < count:
            s = jnp.where(hit, NEG_INF, s)
    return (vals, rank) if want_rank else vals


def _peer_select(q_scr, keys_ref, c_scr, e1_scr, r2_scr, e2_scr, head):
    row0 = pl.multiple_of(head * PEER_DKEY, PEER_DKEY)
    half = PEER_DKEY // 2
    s1_all = jnp.dot(keys_ref[0], q_scr[pl.ds(row0, half), :],
                     precision=lax.Precision.HIGHEST, preferred_element_type=F32)
    s2_all = jnp.dot(keys_ref[1], q_scr[pl.ds(row0 + half, half), :],
                     precision=lax.Precision.HIGHEST, preferred_element_type=F32)
    width = min(LANES, s1_all.shape[1])
    for g in range(s1_all.shape[1] // width):
        cols = slice(g * width, (g + 1) * width)
        _peer_select_group(s1_all[:, cols], s2_all[:, cols], c_scr, e1_scr, r2_scr, e2_scr, head, cols)


def _peer_select_group(s1, s2, c_scr, e1_scr, r2_scr, e2_scr, head, cols):
    a = _top_values(s1, PEER_TOPK)
    b, rank2 = _top_values(s2, PEER_TOPK, want_rank=True)
    s1 = s1 - a[0]
    s2 = s2 - b[0]
    a = [v - a[0] for v in a]
    b = [v - b[0] for v in b]
    cand = [a[i] + b[j] for i in range(PEER_TOPK) for j in range(PEER_TOPK) if (i + 1) * (j + 1) <= PEER_TOPK]
    pad = (-len(cand)) % 8
    cand = jnp.concatenate(cand + [jnp.full_like(cand[0], NEG_INF)] * pad, axis=0)
    top = _top_values(cand, PEER_TOPK)
    cut = top[-1]
    z = jnp.zeros_like(cut)
    for v in top:
        z = z + jnp.exp(v)
    count = jnp.zeros_like(s1)
    for bj in b:
        count = count + jnp.where(s1 + bj >= cut, 1.0, 0.0)
    c_scr[head, :, cols] = count
    e1_scr[head, :, cols] = jnp.exp(s1) / z
    r2_scr[head, :, cols] = rank2.astype(BF16)
    e2_scr[head, :, cols] = jnp.exp(s2).astype(BF16)


def _peer_kernel(x_ref, sc_ref, sh_ref, gate_ref, wq_ref, keys_ref, u_ref, vt_ref, lng_ref, lnb_ref,
                 o_ref, ht_scr, q_scr, c_scr, e1_scr, r2_scr, e2_scr, acc_scr, a_scr, p_scr, *, te, eb):
    j = pl.program_id(1)

    @pl.when(j == 0)
    def _():
        h = x_ref[...] * (1.0 + sc_ref[0]) + sh_ref[0]
        ht_scr[...] = jnp.transpose(h).astype(BF16)
        q_scr[...] = jnp.dot(wq_ref[...], ht_scr[...], preferred_element_type=F32)

        def head_body(head, carry):
            _peer_select(q_scr, keys_ref, c_scr, e1_scr, r2_scr, e2_scr, head)
            return carry
        lax.fori_loop(0, PEER_HEADS, head_body, 0, unroll=2)
        acc_scr[...] = jnp.zeros_like(acc_scr)

    def activations(e):
        a_scr[e % 2] = jnp.dot(u_ref[e * eb:(e + 1) * eb, :], ht_scr[...],
                               preferred_element_type=F32)

    def weigh(e):
        key1 = (j * te + e * eb) // PEER_NKEYS
        for i1 in range(eb // PEER_NKEYS):
            rows = slice(i1 * PEER_NKEYS, (i1 + 1) * PEER_NKEYS)
            w = None
            for head in range(PEER_HEADS):
                row = pl.ds(key1 + i1, 1)
                keep = r2_scr[head] < c_scr[head, row, :].astype(BF16)
                val = jnp.where(keep, e2_scr[head] * e1_scr[head, row, :].astype(BF16), jnp.zeros((), BF16))
                w = val if w is None else w + val
            a_half = a_scr[e % 2, rows, :]
            gelu = a_half * (1.0 + lax.erf(a_half * (2.0 ** 0.5)))
            p_scr[e % 2, rows, :] = w * gelu.astype(BF16)

    def combine(e):
        acc_scr[...] += jnp.dot(vt_ref[:, e * eb:(e + 1) * eb], p_scr[e % 2],
                                preferred_element_type=F32)

    n_blocks = te // eb
    activations(0)
    for e in range(n_blocks):
        if e + 1 < n_blocks:
            activations(e + 1)
        if e > 0:
            combine(e - 1)
        weigh(e)
    combine(n_blocks - 1)

    @pl.when(j == pl.num_programs(1) - 1)
    def _():
        out = jnp.transpose(acc_scr[...])
        z = DN_ALPHA * x_ref[...] + (1.0 + gate_ref[0]) * out
        o_ref[...] = _layernorm_rows(z, lng_ref[...], lnb_ref[...])


def _peer_layer(x, mods, wq_t, keys, u_half, v_t, ln_g, ln_b, seq_len, tm_want=512, te=2048, eb=512):
    shift, scale, gate = mods
    n, d = x.shape
    tm = _row_tile(n, tm_want)
    n_exp = u_half.shape[0]
    sc, sc_blk, sc_idx = _mod_operand(scale, seq_len, tm)
    sh, sh_blk, sh_idx = _mod_operand(shift, seq_len, tm)
    gt, gt_blk, gt_idx = _mod_operand(gate, seq_len, tm)
    hq = PEER_HEADS * PEER_DKEY
    row = lambda f: (lambda i, j: f(i))
    return pl.pallas_call(
        functools.partial(_peer_kernel, te=te, eb=eb),
        grid=(n // tm, n_exp // te),
        in_specs=[pl.BlockSpec((tm, d), lambda i, j: (i, 0)),
                  pl.BlockSpec(sc_blk, row(sc_idx)),
                  pl.BlockSpec(sh_blk, row(sh_idx)),
                  pl.BlockSpec(gt_blk, row(gt_idx)),
                  pl.BlockSpec((hq, d), lambda i, j: (0, 0)),
                  pl.BlockSpec((2, PEER_NKEYS, PEER_DKEY // 2), lambda i, j: (0, 0, 0)),
                  pl.BlockSpec((te, d), lambda i, j: (j, 0)),
                  pl.BlockSpec((d, te), lambda i, j: (0, j)),
                  pl.BlockSpec((1, d), lambda i, j: (0, 0)),
                  pl.BlockSpec((1, d), lambda i, j: (0, 0))],
        out_specs=pl.BlockSpec((tm, d), lambda i, j: (i, 0)),
        out_shape=jax.ShapeDtypeStruct((n, d), F32),
        scratch_shapes=[pltpu.VMEM((d, tm), BF16),
                        pltpu.VMEM((hq, tm), F32),
                        pltpu.VMEM((PEER_HEADS, PEER_NKEYS, tm), F32),
                        pltpu.VMEM((PEER_HEADS, PEER_NKEYS, tm), F32),
                        pltpu.VMEM((PEER_HEADS, PEER_NKEYS, tm), BF16),
                        pltpu.VMEM((PEER_HEADS, PEER_NKEYS, tm), BF16),
                        pltpu.VMEM((d, tm), F32),
                        pltpu.VMEM((2, eb, tm), F32),
                        pltpu.VMEM((2, eb, tm), BF16)],
        compiler_params=_cparams("parallel", "arbitrary"),
        name="peer",
    )(x, sc, sh, gt, wq_t, keys, u_half, v_t, ln_g.reshape(1, d), ln_b.reshape(1, d))


def _block_mean_kernel(*refs, n_parts):
    o_ref = refs[-1]
    total = None
    rows = 0
    for p_ref in refs[-1 - n_parts:-1]:
        part = jnp.sum(p_ref[0].astype(F32), axis=0, keepdims=True)
        total = part if total is None else total + part
        rows += p_ref.shape[1]
    o_ref[0] = total * (1.0 / rows)


def _block_means_dense(k, batch, seq_len):
    w = k.shape[1]
    nb = seq_len // MOBA_BLOCK
    out = pl.pallas_call(
        functools.partial(_block_mean_kernel, n_parts=1),
        grid=(batch * nb,),
        in_specs=[pl.BlockSpec((1, MOBA_BLOCK, w), lambda i: (i, 0, 0))],
        out_specs=pl.BlockSpec((1, 1, w), lambda i: (i, 0, 0)),
        out_shape=jax.ShapeDtypeStruct((batch * nb, 1, w), F32),
        compiler_params=_cparams("parallel"),
        name="block_means",
    )(k.reshape(batch * nb, MOBA_BLOCK, w))
    return out.reshape(batch, nb, w)


def _moba_prompt_kernel(qi_ref, kv0_ref, kv1_ref, q_ref, k0_ref, v0_ref, k1_ref, v1_ref, mean_ref, o_ref,
                        m_scr, acc_scr, qa_scr):
    t = pl.program_id(1)
    qi = qi_ref[t]
    kv0 = kv0_ref[t]
    kv1 = kv1_ref[t]
    own = kv0 == qi
    blk = MOBA_BLOCK
    masked = 1e30
    row = lax.broadcasted_iota(jnp.int32, (blk, blk), 0)
    col = lax.broadcasted_iota(jnp.int32, (blk, blk), 1)
    causal_bias = jnp.where(own, jnp.where(col <= row, 0.0, NEG_INF), 0.0)
    gcol_i = lax.broadcasted_iota(jnp.int32, (blk, LANES), 1)
    head_cols = [slice(h * HEAD_DIM, (h + 1) * HEAD_DIM) for h in range(MOBA_HEADS)]

    @pl.when(own)
    def _():
        n_rows = mean_ref.shape[1]
        grow = lax.broadcasted_iota(jnp.int32, (n_rows, blk), 0).astype(F32)
        n_past = qi.astype(F32)
        for h, cs in enumerate(head_cols):
            g = lax.dot_general(mean_ref[0, :, cs], q_ref[0, :, cs].astype(F32), (((1,), (1,)), ((), ())),
                                precision=lax.Precision.HIGHEST, preferred_element_type=F32)
            g = jnp.where(grow < n_past, g, NEG_INF)
            bias_t = jnp.where(grow == n_past, 0.0, -masked)
            for r in range(MOBA_TOPK):
                best = jnp.max(g, axis=0, keepdims=True)
                pick = jnp.min(jnp.where(g == best, grow, float(n_rows)), axis=0, keepdims=True)
                hit = grow == pick
                bias_t = jnp.where(r < qi, jnp.where(hit, 0.0, bias_t), bias_t)
                g = jnp.where(hit, NEG_INF, g)
            if n_rows < LANES:
                bias_t = jnp.concatenate([bias_t, jnp.full((LANES - n_rows, blk), -masked, F32)], axis=0)
            qa_scr[h] = jnp.concatenate([q_ref[0, :, cs], jnp.transpose(bias_t).astype(BF16)], axis=1)
            m_scr[h] = jnp.full((blk, LANES), NEG_INF, F32)
            acc_scr[h] = jnp.zeros((blk, 2 * HEAD_DIM), F32)

    onehot0 = jnp.where(gcol_i == kv0, 1.0, 0.0).astype(BF16)
    onehot1 = jnp.where(gcol_i == kv1, 1.0, 0.0).astype(BF16)
    absent1 = jnp.where(kv1 < 0, -masked, 0.0)
    ones = jnp.ones((blk, HEAD_DIM), BF16)
    twice = lambda a: jnp.concatenate([a, a], axis=1)
    nt = (((1,), (1,)), ((), ()))
    for h, cs in enumerate(head_cols):
        qa = qa_scr[h]
        s0 = lax.dot_general(qa, jnp.concatenate([k0_ref[0, :, cs], onehot0], axis=1), nt,
                             preferred_element_type=F32) + causal_bias
        s1 = lax.dot_general(qa, jnp.concatenate([k1_ref[0, :, cs], onehot1], axis=1), nt,
                             preferred_element_type=F32) + absent1
        m_old = m_scr[h]
        m_new = jnp.maximum(m_old, jnp.maximum(jnp.max(s0, axis=-1, keepdims=True),
                                               jnp.max(s1, axis=-1, keepdims=True)))
        alpha = jnp.exp(m_old - m_new)
        p0 = jnp.exp(s0 - twice(m_new)).astype(BF16)
        p1 = jnp.exp(s1 - twice(m_new)).astype(BF16)
        acc_scr[h] = (twice(alpha) * acc_scr[h]
                      + jnp.dot(p0, jnp.concatenate([v0_ref[0, :, cs], ones], axis=1), preferred_element_type=F32)
                      + jnp.dot(p1, jnp.concatenate([v1_ref[0, :, cs], ones], axis=1), preferred_element_type=F32))
        m_scr[h] = m_new

    @pl.when(kv0 <= 1)
    def _():
        for h, cs in enumerate(head_cols):
            acc = acc_scr[h]
            o_ref[0, :, cs] = (acc[:, :HEAD_DIM] / acc[:, HEAD_DIM:]).astype(o_ref.dtype)


def _moba_prompt(q, k, v, means, batch, seq_len):
    w = q.shape[1]
    nb = seq_len // MOBA_BLOCK
    assert nb <= LANES
    n_rows = -(-nb // 8) * 8
    means = jnp.pad(means, ((0, 0), (0, n_rows - nb), (0, 0)))
    steps = [(qi, kv, kv - 1) for qi in range(nb) for kv in range(qi, -1, -2)]
    qi_tab, kv0_tab, kv1_tab = (jnp.asarray(col, jnp.int32) for col in zip(*steps))
    r3 = lambda a: a.reshape(batch, seq_len, w)
    q_spec = pl.BlockSpec((1, MOBA_BLOCK, w), lambda b, t, qt, k0, k1: (b, qt[t], 0))
    kv0_spec = pl.BlockSpec((1, MOBA_BLOCK, w), lambda b, t, qt, k0, k1: (b, k0[t], 0))
    kv1_spec = pl.BlockSpec((1, MOBA_BLOCK, w), lambda b, t, qt, k0, k1: (b, jnp.maximum(k1[t], 0), 0))
    out = pl.pallas_call(
        _moba_prompt_kernel,
        grid_spec=pltpu.PrefetchScalarGridSpec(
            num_scalar_prefetch=3,
            grid=(batch, len(steps)),
            in_specs=[q_spec, kv0_spec, kv0_spec, kv1_spec, kv1_spec,
                      pl.BlockSpec((1, n_rows, w), lambda b, t, qt, k0, k1: (b, 0, 0))],
            out_specs=q_spec,
            scratch_shapes=[pltpu.VMEM((MOBA_HEADS, MOBA_BLOCK, LANES), F32),
                            pltpu.VMEM((MOBA_HEADS, MOBA_BLOCK, 2 * HEAD_DIM), F32),
                            pltpu.VMEM((MOBA_HEADS, MOBA_BLOCK, HEAD_DIM + LANES), BF16)]),
        out_shape=jax.ShapeDtypeStruct((batch, seq_len, w), BF16),
        compiler_params=_cparams("parallel", "arbitrary"),
        name="moba_prompt",
    )(qi_tab, kv0_tab, kv1_tab, r3(q), r3(k), r3(v), r3(k), r3(v), means)
    return out.reshape(batch * seq_len, w)


def _moba_decode_kernel(pt_ref, q_ref, kn_ref, vn_ref, *rest, n_new, n_past, ppb):
    k_pages = rest[:ppb]
    v_pages = rest[ppb:2 * ppb]
    o_ref, g_scr, m_scr, l_scr, acc_scr = rest[2 * ppb:]
    j = pl.program_id(1)
    nq = MOBA_HEADS * n_new
    head_bits = MOBA_HEADS.bit_length() - 1
    new_bits = n_new.bit_length() - 1
    q = q_ref[0]
    qb = q.astype(BF16)

    def same_head(n_cols):
        lane_head = lax.broadcasted_iota(jnp.int32, (nq, n_cols), 1) & (MOBA_HEADS - 1)
        row_head = lax.broadcasted_iota(jnp.int32, (nq, n_cols), 0) >> new_bits
        return lane_head == row_head

    def scores(keys):
        return lax.dot_general(qb, keys, (((1,), (1,)), ((), ())), preferred_element_type=F32)

    page_len = k_pages[0].shape[1]
    page_rows = page_len * MOBA_HEADS
    pages_per_block = MOBA_BLOCK // page_len
    head_bias = jnp.where(same_head(page_rows), 0.0, NEG_INF)
    for bi in range(ppb // pages_per_block):
        block = j * (ppb // pages_per_block) + bi
        k_blk = k_pages[bi * pages_per_block:(bi + 1) * pages_per_block]
        v_blk = v_pages[bi * pages_per_block:(bi + 1) * pages_per_block]
        key_sum = None
        for k_ref in k_blk:
            part = jnp.sum(k_ref[0], axis=0)
            key_sum = part if key_sum is None else key_sum + part
        gate = lax.dot_general(q, key_sum * (1.0 / MOBA_BLOCK), (((1,), (1,)), ((), ())),
                               precision=lax.Precision.HIGHEST, preferred_element_type=F32)
        g_scr[block] = jnp.sum(jnp.where(same_head(MOBA_HEADS), gate, 0.0), axis=-1, keepdims=True)
        s_pages = [scores(k_ref[0].reshape(page_rows, HEAD_DIM).astype(BF16)) + head_bias for k_ref in k_blk]
        m = None
        for s in s_pages:
            s_max = jnp.max(s, axis=-1, keepdims=True)
            m = s_max if m is None else jnp.maximum(m, s_max)
        l = None
        acc = None
        for s, v_ref in zip(s_pages, v_blk):
            p = jnp.exp(s - m)
            p_sum = jnp.sum(p, axis=-1, keepdims=True)
            pv = jnp.dot(p.astype(BF16), v_ref[0].reshape(page_rows, HEAD_DIM).astype(BF16),
                         preferred_element_type=F32)
            l = p_sum if l is None else l + p_sum
            acc = pv if acc is None else acc + pv
        m_scr[block] = m
        l_scr[block] = l
        acc_scr[block] = acc

    @pl.when(j == pl.num_programs(1) - 1)
    def _():
        s = scores(kn_ref[0].astype(BF16))
        key_t = lax.broadcasted_iota(jnp.int32, s.shape, 1) >> head_bits
        qry_r = lax.broadcasted_iota(jnp.int32, s.shape, 0) & (n_new - 1)
        s = jnp.where(same_head(s.shape[1]), jnp.where(key_t <= qry_r, s, NEG_INF), NEG_INF)
        m_own = jnp.max(s, axis=-1, keepdims=True)
        p = jnp.exp(s - m_own)
        l_own = jnp.sum(p, axis=-1, keepdims=True)
        acc_own = jnp.dot(p.astype(BF16), vn_ref[0].astype(BF16), preferred_element_type=F32)

        k_sel = min(MOBA_TOPK, n_past)
        gates = [g_scr[b] for b in range(n_past)]
        weights = []
        m_all = m_own
        for b in range(n_past):
            place = jnp.zeros_like(gates[b])
            for o in range(n_past):
                if o != b:
                    ahead = (gates[o] >= gates[b]) if o < b else (gates[o] > gates[b])
                    place = place + jnp.where(ahead, 1.0, 0.0)
            weights.append(jnp.where(place < k_sel, 0.0, NEG_INF))
            m_all = jnp.maximum(m_all, m_scr[b] + weights[b])
        w_own = jnp.exp(m_own - m_all)
        l_all = w_own * l_own
        acc_all = w_own * acc_own
        for b in range(n_past):
            w_b = jnp.exp(m_scr[b] + weights[b] - m_all)
            l_all = l_all + w_b * l_scr[b]
            acc_all = acc_all + w_b * acc_scr[b]
        o_ref[0] = acc_all / l_all


def _moba_decode(q, k_new, v_new, cache_k, cache_v, page_table_flat, batch, n_new, pages_per_seq):
    _, page, heads, hd = cache_k.shape
    assert heads & (heads - 1) == 0 and n_new & (n_new - 1) == 0
    n_past = pages_per_seq * page // MOBA_BLOCK
    blocks_per_step = next(c for c in (4, 2, 1) if n_past % c == 0)
    ppb = (MOBA_BLOCK // page) * blocks_per_step
    nq = heads * n_new
    q_hq = q.reshape(batch, n_new, heads, hd).transpose(0, 2, 1, 3).reshape(batch, nq, hd)
    kv_rows = lambda a: a.reshape(batch, nq, hd)
    seq_spec = pl.BlockSpec((1, nq, hd), lambda b, j, pt: (b, 0, 0))
    page_spec = lambda part: pl.BlockSpec(
        (1, page, heads, hd), lambda b, j, pt: (pt[b * pages_per_seq + j * ppb + part], 0, 0, 0))
    out = pl.pallas_call(
        functools.partial(_moba_decode_kernel, n_new=n_new, n_past=n_past, ppb=ppb),
        grid_spec=pltpu.PrefetchScalarGridSpec(
            num_scalar_prefetch=1,
            grid=(batch, n_past // blocks_per_step),
            in_specs=[seq_spec, seq_spec, seq_spec] + [page_spec(p) for p in range(ppb)] * 2,
            out_specs=seq_spec,
            scratch_shapes=[pltpu.VMEM((n_past, nq, 1), F32),
                            pltpu.VMEM((n_past, nq, 1), F32),
                            pltpu.VMEM((n_past, nq, 1), F32),
                            pltpu.VMEM((n_past, nq, hd), F32)]),
        out_shape=jax.ShapeDtypeStruct((batch, nq, hd), F32),
        compiler_params=_cparams("parallel", "arbitrary"),
        name="moba_decode",
    )(page_table_flat, q_hq, kv_rows(k_new), kv_rows(v_new),
      *([cache_k] * ppb), *([cache_v] * ppb))
    return out.reshape(batch, heads, n_new, hd).transpose(0, 2, 1, 3).reshape(batch * n_new, heads * hd)


def _rope_tables(pos):
    half = HEAD_DIM // 2
    inv = ROPE_THETA ** (-jnp.arange(half, dtype=F32) / half)
    ang = pos.astype(F32)[:, None] * inv[None, :]
    cos, sin = jnp.cos(ang), jnp.sin(ang)
    return jnp.concatenate([cos, cos], axis=-1), jnp.concatenate([-sin, sin], axis=-1)


def _trunk(x, mods, mods_kv, s0, past, p, batch, seq_len):
    pos0 = 0 if past is None else past[3] * past[0].shape[1]
    pos = pos0 + jnp.arange(seq_len, dtype=jnp.int32)
    cos, sin = _rope_tables(pos)
    n = batch * seq_len
    if seq_len % _row_tile(n, 512):
        cos, sin = jnp.tile(cos, (batch, 1)), jnp.tile(sin, (batch, 1))
    gla_fin = []
    k_new = v_new = None
    dm = MOBA_HEADS * HEAD_DIM
    for layer in range(DEPTH):
        shift, scale, gate = mods[layer][0]
        if layer < N_A:
            x, s_fin = _gla_layer(x, mods[layer][0], s0[layer], p["gla_w_in"][layer], p["gla_w_g2"][layer],
                                  p["gla_b_g"][layer], p["gla_norm_g"][layer], p["gla_w_o"][layer],
                                  p["ln_g"][layer, 0], p["ln_b"][layer, 0], batch, seq_len)
            gla_fin.append(s_fin)
        else:
            jb = layer - N_A
            (q,) = _proj(x, scale, shift, p["moba_w_q"][jb], seq_len, [(BF16 if past is None else F32, 0, dm)],
                         rope=(cos, sin), rope_heads=MOBA_HEADS, out_scale=HEAD_DIM ** -0.5)
            if past is None:
                o = _moba_prompt(q, k_bf16, v_bf16, means, batch, seq_len)
            else:
                o = _moba_decode(q, k_new, v_new, past[0], past[1], past[2], batch, seq_len,
                                 past[3]).astype(BF16)
            x = _out_ln(o, p["moba_w_o"][jb], x, gate, p["ln_g"][layer, 0], p["ln_b"][layer, 0], seq_len)
        x = _peer_layer(x, mods[layer][1], p["peer_w_q"][layer], p["peer_keys"][layer], p["peer_u"][layer],
                        p["peer_v"][layer], p["ln_g"][layer, 1], p["ln_b"][layer, 1], seq_len)
        if layer == N_A - 1:
            kv_shift, kv_scale = mods_kv
            kv_cols = [(F32, 0, dm), (F32, dm, 2 * dm)]
            if past is None:
                kv_cols += [(BF16, 0, dm), (BF16, dm, 2 * dm)]
            kv = _proj(x, kv_scale, kv_shift, p["w_kv"], seq_len, kv_cols, rope=(cos, sin),
                       rope_heads=MOBA_HEADS)
            k_new, v_new = kv[:2]
            if past is None:
                k_bf16, v_bf16 = kv[2:]
                means = _block_means_dense(k_new, batch, seq_len)
    return x, jnp.stack(gla_fin), k_new, v_new


def kernel(x_prompt, x_sample, c_prompt, c_sample, state_gla, cache_k, cache_v, page_table,
           w_ada, b_ada, ln_g, ln_b, gla_w_in, gla_w_g2, gla_b_g, gla_norm_g, gla_w_o,
           w_ada_kv, b_ada_kv, w_kv, moba_w_q, moba_w_o, peer_w_q, peer_keys, peer_u, peer_v):
    b_p, l_p, d = x_prompt.shape
    b_s, l_s, _ = x_sample.shape
    gla_cols = -(-gla_w_in.shape[-1] // LANES) * LANES
    p = {
        "ln_g": ln_g, "ln_b": ln_b,
        "gla_w_in": jnp.pad(gla_w_in, ((0, 0), (0, 0), (0, gla_cols - gla_w_in.shape[-1]))).astype(BF16),
        "gla_w_g2": jnp.pad(gla_w_g2, ((0, 0), (0, LANES - GLA_GATE_RANK), (0, 0))),
        "gla_b_g": gla_b_g, "gla_norm_g": gla_norm_g,
        "gla_w_o": gla_w_o.astype(BF16),
        "w_kv": w_kv.astype(BF16),
        "moba_w_q": moba_w_q.astype(BF16), "moba_w_o": moba_w_o.astype(BF16),
        "peer_w_q": jnp.swapaxes(peer_w_q, 1, 2).astype(BF16),
        "peer_keys": peer_keys,
        "peer_u": (0.5 * peer_u).astype(BF16),
        "peer_v": jnp.swapaxes(peer_v, 1, 2).astype(BF16),
    }
    c_all = jnp.concatenate([c_prompt, c_sample], axis=0)
    rows = c_all.shape[0]
    c_all = jnp.pad(c_all, ((0, (-rows) % 8), (0, 0)))
    m_all = _ada_all(c_all, w_ada.reshape(DEPTH * 2, d, 3 * d), b_ada.reshape(DEPTH * 2, 1, 3 * d))
    m_kv = _ada_all(c_all, w_ada_kv[None], b_ada_kv[None, None])[0]

    def group_mods(lo, hi):
        mods = [[tuple(m_all[layer * 2 + sub, lo:hi, i * d:(i + 1) * d] for i in range(3))
                 for sub in range(2)] for layer in range(DEPTH)]
        return mods, (m_kv[lo:hi, :d], m_kv[lo:hi, d:])

    mods_p, kv_p = group_mods(0, b_p)
    mods_s, kv_s = group_mods(b_p, b_p + b_s)

    zero_state = jnp.zeros((N_A, b_p) + state_gla.shape[2:], F32)
    y_p, gla_p, k_p, v_p = _trunk(x_prompt.reshape(b_p * l_p, d), mods_p, kv_p, zero_state, None, p, b_p, l_p)

    past = (cache_k, cache_v, page_table.reshape(-1).astype(jnp.int32), page_table.shape[1])
    y_s, gla_s, k_s, v_s = _trunk(x_sample.reshape(b_s * l_s, d), mods_s, kv_s, state_gla, past, p, b_s, l_s)

    heads = lambda a, b, l: a.reshape(b, l, MOBA_HEADS, HEAD_DIM)
    return (y_p.reshape(b_p, l_p, d), y_s.reshape(b_s, l_s, d), gla_p, gla_s,
            heads(k_p, b_p, l_p), heads(v_p, b_p, l_p), heads(k_s, b_s, l_s), heads(v_s, b_s, l_s))
```

```python
import functools
import math

import jax
import jax.numpy as jnp
from jax import lax
from jax.experimental import pallas as pl
from jax.experimental.pallas import tpu as pltpu

F32 = jnp.float32
BF16 = jnp.bfloat16

D_MODEL = 1024
DEPTH = 4
N_A = DEPTH // 2
N_B = DEPTH - N_A
GLA_HEADS = 4
GLA_DK = D_MODEL // 2 // GLA_HEADS
GLA_DV = D_MODEL // GLA_HEADS
GLA_GATE_RANK = 16
GLA_TAU = 16.0
MOBA_HEADS = 8
HEAD_DIM = D_MODEL // MOBA_HEADS
MOBA_BLOCK = 256
MOBA_TOPK = 3
ROPE_THETA = 10000.0
PEER_HEADS = 8
PEER_NKEYS = 128
PEER_EXPERTS = PEER_NKEYS * PEER_NKEYS
PEER_DKEY = 256
PEER_TOPK = 16
DN_ALPHA = (2.0 * DEPTH) ** 0.25
LN_EPS = 1e-5

LANES = 128
VMEM_LIMIT = 56 * 1024 * 1024
NEG_INF = float("-inf")


def _cparams(*sem):
    return pltpu.CompilerParams(dimension_semantics=sem, vmem_limit_bytes=VMEM_LIMIT)


def _row_tile(n_rows, want):
    t = min(want, n_rows)
    while n_rows % t:
        t //= 2
    return t


def _mod_operand(m, seq_len, tm):
    d = m.shape[-1]
    if seq_len % tm == 0:
        per = seq_len // tm
        return m[:, None, :], (1, 1, d), (lambda i: (i // per, 0, 0))
    rows = jnp.repeat(m, seq_len, axis=0)
    return rows.reshape(rows.shape[0] // tm, tm, d), (1, tm, d), (lambda i: (i, 0, 0))


def _layernorm_rows(z, g, b):
    mu = jnp.mean(z, axis=-1, keepdims=True)
    zc = z - mu
    var = jnp.mean(zc * zc, axis=-1, keepdims=True)
    return zc * lax.rsqrt(var + LN_EPS) * g + b


def _ada_kernel(c_ref, w_ref, b_ref, o_ref):
    c = c_ref[...]
    s = c * jax.nn.sigmoid(c)
    o_ref[0] = jnp.dot(s.astype(BF16), w_ref[0].astype(BF16), preferred_element_type=F32) + b_ref[0]


def _ada_all(c, w, b):
    g, d, n = w.shape
    r = c.shape[0]
    tn = 1024
    return pl.pallas_call(
        _ada_kernel,
        grid=(g, n // tn),
        in_specs=[pl.BlockSpec((r, d), lambda gi, j: (0, 0)),
                  pl.BlockSpec((1, d, tn), lambda gi, j: (gi, 0, j)),
                  pl.BlockSpec((1, 1, tn), lambda gi, j: (gi, 0, j))],
        out_specs=pl.BlockSpec((1, r, tn), lambda gi, j: (gi, 0, j)),
        out_shape=jax.ShapeDtypeStruct((g, r, n), F32),
        compiler_params=_cparams("parallel", "parallel"),
        name="ada",
    )(c, w, b)


def _rope_cols(y, cos, sin, heads):
    outs = []
    for hh in range(heads):
        yh = y[:, hh * HEAD_DIM:(hh + 1) * HEAD_DIM]
        outs.append(yh * cos + pltpu.roll(yh, HEAD_DIM // 2, axis=1) * sin)
    return outs


def _proj_kernel(x_ref, sc_ref, sh_ref, w_ref, *rest, rope_heads, out_cols, out_scale):
    if rope_heads:
        cos_ref, sin_ref = rest[:2]
        outs = rest[2:]
    else:
        outs = rest
    h = x_ref[...] * (1.0 + sc_ref[0]) + sh_ref[0]
    y = jnp.dot(h.astype(BF16), w_ref[...], preferred_element_type=F32)
    if out_scale != 1.0:
        y = y * out_scale
    pieces = []
    if rope_heads:
        parts = _rope_cols(y, cos_ref[...], sin_ref[...], rope_heads)
        pieces += [(hh * HEAD_DIM, p) for hh, p in enumerate(parts)]
    plain = rope_heads * HEAD_DIM
    for lo, hi in sorted({(max(lo, plain), hi) for _, lo, hi in out_cols if hi > plain}):
        pieces.append((lo, y[:, lo:hi]))
    for o, (_, lo, hi) in zip(outs, out_cols):
        for start, vals in pieces:
            if lo <= start and start + vals.shape[1] <= hi:
                o[:, start - lo:start - lo + vals.shape[1]] = vals.astype(o.dtype)


def _proj(x, scale, shift, w_bf16, seq_len, out_cols, rope=None, rope_heads=0, out_scale=1.0, tm_want=512):
    n, d = x.shape
    n_out = w_bf16.shape[1]
    tm = _row_tile(n, tm_want)
    sc, sc_blk, sc_idx = _mod_operand(scale, seq_len, tm)
    sh, sh_blk, sh_idx = _mod_operand(shift, seq_len, tm)
    in_specs = [pl.BlockSpec((tm, d), lambda i: (i, 0)),
                pl.BlockSpec(sc_blk, sc_idx),
                pl.BlockSpec(sh_blk, sh_idx),
                pl.BlockSpec((d, n_out), lambda i: (0, 0))]
    args = [x, sc, sh, w_bf16]
    if rope_heads:
        cos, sin = rope
        per = cos.shape[0] // tm
        in_specs += [pl.BlockSpec((tm, HEAD_DIM), lambda i: (i % per, 0)),
                     pl.BlockSpec((tm, HEAD_DIM), lambda i: (i % per, 0))]
        args += [cos, sin]
    outs = pl.pallas_call(
        functools.partial(_proj_kernel, rope_heads=rope_heads, out_cols=tuple(out_cols), out_scale=out_scale),
        grid=(n // tm,),
        in_specs=in_specs,
        out_specs=[pl.BlockSpec((tm, hi - lo), lambda i: (i, 0)) for _, lo, hi in out_cols],
        out_shape=[jax.ShapeDtypeStruct((n, hi - lo), dt) for dt, lo, hi in out_cols],
        compiler_params=_cparams("parallel"),
        name="proj",
    )(*args)
    return outs


def _out_ln_kernel(a_ref, w_ref, x_ref, gate_ref, g_ref, b_ref, o_ref):
    y = jnp.dot(a_ref[...], w_ref[...], preferred_element_type=F32)
    z = DN_ALPHA * x_ref[...] + (1.0 + gate_ref[0]) * y
    o_ref[...] = _layernorm_rows(z, g_ref[...], b_ref[...])


def _out_ln(a_bf16, w_bf16, x, gate, ln_g, ln_b, seq_len, tm_want=512):
    n, d = x.shape
    k = a_bf16.shape[1]
    tm = _row_tile(n, tm_want)
    gt, gt_blk, gt_idx = _mod_operand(gate, seq_len, tm)
    return pl.pallas_call(
        _out_ln_kernel,
        grid=(n // tm,),
        in_specs=[pl.BlockSpec((tm, k), lambda i: (i, 0)),
                  pl.BlockSpec((k, d), lambda i: (0, 0)),
                  pl.BlockSpec((tm, d), lambda i: (i, 0)),
                  pl.BlockSpec(gt_blk, gt_idx),
                  pl.BlockSpec((1, d), lambda i: (0, 0)),
                  pl.BlockSpec((1, d), lambda i: (0, 0))],
        out_specs=pl.BlockSpec((tm, d), lambda i: (i, 0)),
        out_shape=jax.ShapeDtypeStruct((n, d), F32),
        compiler_params=_cparams("parallel"),
        name="out_ln",
    )(a_bf16, w_bf16, x, gt, ln_g.reshape(1, d), ln_b.reshape(1, d))


def _log_sigmoid(z):
    return jnp.minimum(z, 0.0) - jnp.log(1.0 + jnp.exp(-jnp.abs(z)))


def _gla_kernel(q_ref, k_ref, v_ref, r_ref, g1_ref, wg2_ref, bg_ref, ng_ref, s0_ref,
                og_ref, sfin_ref, s_scr, *, chunk, sub, seq_len, padded_len):
    c = pl.program_id(1)
    n_chunks = padded_len // chunk

    @pl.when(c == 0)
    def _():
        s_scr[...] = s0_ref[0]

    z = jnp.dot(g1_ref[0], wg2_ref[...], precision=lax.Precision.HIGHEST,
                preferred_element_type=F32) + bg_ref[...]
    la = _log_sigmoid(z) * (1.0 / GLA_TAU)
    live = None
    if padded_len != seq_len:
        row = c * chunk + lax.broadcasted_iota(jnp.int32, (chunk, 1), 0)
        live = row < seq_len
        la = jnp.where(live, la, 0.0)
    n_live = min(chunk, seq_len) if n_chunks == 1 else chunk

    ri = lax.broadcasted_iota(jnp.int32, (chunk, chunk), 0)
    ci = lax.broadcasted_iota(jnp.int32, (chunk, chunk), 1)
    tri = (ri >= ci).astype(F32)
    bc_all = jnp.dot(tri, la, precision=lax.Precision.HIGHEST, preferred_element_type=F32)

    for h in range(GLA_HEADS):
        kcols = slice(h * GLA_DK, (h + 1) * GLA_DK)
        vcols = slice(h * GLA_DV, (h + 1) * GLA_DV)
        _gla_head(q_ref[0, :, kcols] * (GLA_DK ** -0.5), k_ref[0, :, kcols], v_ref[0, :, vcols],
                  r_ref[0, :, vcols], bc_all[:, kcols], live, ng_ref, og_ref, vcols, s_scr, h,
                  chunk=chunk, sub=sub, n_live=n_live)

    @pl.when(c == n_chunks - 1)
    def _():
        sfin_ref[0] = s_scr[...]


def _gla_head(q, k, v, r, bc, live, ng_ref, og_ref, vcols, s_scr, h, *, chunk, sub, n_live):
    if live is not None:
        k = jnp.where(live, k, 0.0)
        v = jnp.where(live, v, 0.0)
    s = s_scr[h]
    o = jnp.dot((q * jnp.exp(bc)).astype(BF16), s.astype(BF16), preferred_element_type=F32)
    sub_row = lax.broadcasted_iota(jnp.int32, (sub, 1), 0)
    sub_col = lax.broadcasted_iota(jnp.int32, (8, sub), 1)
    o_parts = []
    for blk in range(chunk // sub):
        lo = blk * sub
        b_i = bc[lo:lo + sub]
        q_i = q[lo:lo + sub]
        o_i = o[lo:lo + sub]
        if blk > 0:
            b_ref = bc[lo - 1:lo]
            q_t = q_i * jnp.exp(b_i - b_ref)
            k_t = k[:lo] * jnp.exp(b_ref - bc[:lo])
            att = lax.dot_general(q_t.astype(BF16), k_t.astype(BF16), (((1,), (1,)), ((), ())),
                                  preferred_element_type=F32)
            o_i = o_i + jnp.dot(att.astype(BF16), v[:lo].astype(BF16), preferred_element_type=F32)
        groups = [(a, min(a + 8, sub)) for a in range(0, sub, 8)]
        att_g = [jnp.zeros((b - a, sub), F32) for a, b in groups]
        for j in range(min(sub, max(n_live - lo, 0))):
            b_j = bc[lo + j:lo + j + 1]
            k_j = k[lo + j:lo + j + 1]
            for gi, (a, b) in enumerate(groups):
                if b <= j:
                    continue
                d = b_i[a:b] - b_j
                if a < j:
                    d = jnp.where(sub_row[a:b] >= j, d, NEG_INF)
                w_col = jnp.sum(q_i[a:b] * k_j * jnp.exp(d), axis=-1, keepdims=True)
                att_g[gi] = jnp.where(sub_col[:b - a] == j, w_col, att_g[gi])
        att_own = jnp.concatenate(att_g, axis=0) if len(att_g) > 1 else att_g[0]
        o_parts.append(o_i + jnp.dot(att_own.astype(BF16), v[lo:lo + sub].astype(BF16),
                                     preferred_element_type=F32))
    o = jnp.concatenate(o_parts, axis=0) if len(o_parts) > 1 else o_parts[0]

    o = o * lax.rsqrt(jnp.mean(o * o, axis=-1, keepdims=True) + LN_EPS) * ng_ref[...]
    og_ref[0, :, vcols] = (o * (r * jax.nn.sigmoid(r))).astype(og_ref.dtype)

    b_last = bc[chunk - 1:chunk]
    k_d = k * jnp.exp(b_last - bc)
    decay = jnp.transpose(jnp.broadcast_to(jnp.exp(b_last), (GLA_DK, GLA_DK)))
    kv = lax.dot_general(k_d.astype(BF16), v.astype(BF16), (((0,), (0,)), ((), ())),
                         preferred_element_type=F32)
    s_scr[h] = jnp.concatenate([decay] * (GLA_DV // GLA_DK), axis=1) * s + kv


def _gla_scan(proj, s0, wg2_pad, b_g, norm_g, batch, seq_len):
    cols = proj.shape[1]
    if seq_len >= 64:
        chunk, sub = 64, 16
    else:
        chunk = sub = 16
    padded = -(-seq_len // chunk) * chunk
    p3 = proj.reshape(batch, seq_len, cols)
    if padded != seq_len:
        p3 = jnp.pad(p3, ((0, 0), (0, padded - seq_len), (0, 0)))
    n_chunks = padded // chunk
    dq = GLA_HEADS * GLA_DK
    dv = GLA_HEADS * GLA_DV
    k0, v0, r0, g0 = 1, 2 * dq // dv, (2 * dq + dv) // dv, (2 * dq + 2 * dv) // LANES
    state_spec = pl.BlockSpec((1, GLA_HEADS, GLA_DK, GLA_DV), lambda b, c: (b, 0, 0, 0))
    og, s_fin = pl.pallas_call(
        functools.partial(_gla_kernel, chunk=chunk, sub=sub, seq_len=seq_len, padded_len=padded),
        grid=(batch, n_chunks),
        in_specs=[pl.BlockSpec((1, chunk, dq), lambda b, c: (b, c, 0)),
                  pl.BlockSpec((1, chunk, dq), lambda b, c: (b, c, k0)),
                  pl.BlockSpec((1, chunk, dv), lambda b, c: (b, c, v0)),
                  pl.BlockSpec((1, chunk, dv), lambda b, c: (b, c, r0)),
                  pl.BlockSpec((1, chunk, LANES), lambda b, c: (b, c, g0)),
                  pl.BlockSpec((LANES, dq), lambda b, c: (0, 0)),
                  pl.BlockSpec((1, dq), lambda b, c: (0, 0)),
                  pl.BlockSpec((1, GLA_DV), lambda b, c: (0, 0)),
                  state_spec],
        out_specs=[pl.BlockSpec((1, chunk, dv), lambda b, c: (b, c, 0)), state_spec],
        out_shape=[jax.ShapeDtypeStruct((batch, padded, dv), BF16),
                   jax.ShapeDtypeStruct((batch, GLA_HEADS, GLA_DK, GLA_DV), F32)],
        scratch_shapes=[pltpu.VMEM((GLA_HEADS, GLA_DK, GLA_DV), F32)],
        compiler_params=_cparams("parallel", "arbitrary"),
        name="gla_scan",
    )(p3, p3, p3, p3, p3, wg2_pad, b_g.reshape(1, dq), norm_g.reshape(1, GLA_DV), s0)
    og = og[:, :seq_len].reshape(batch * seq_len, GLA_HEADS * GLA_DV)
    return og, s_fin


def _gla_layer(x, mods, s0, w_in_pad, wg2_pad, b_g, norm_g, w_o, ln_g, ln_b, batch, seq_len):
    shift, scale, gate = mods
    (proj,) = _proj(x, scale, shift, w_in_pad, seq_len, [(F32, 0, w_in_pad.shape[1])])
    og, s_fin = _gla_scan(proj, s0, wg2_pad, b_g, norm_g, batch, seq_len)
    return _out_ln(og, w_o, x, gate, ln_g, ln_b, seq_len), s_fin


def _top_values(s, count, want_rank=False):
    vals = []
    rank = jnp.full(s.shape, float(count), F32) if want_rank else None
    for r in range(count):
        m = jnp.max(s, axis=0, keepdims=True)
        vals.append(m)
        hit = s == m
        if want_rank:
            rank = jnp.where(hit, float(r), rank)
        if r + 1 < count:
            s = jnp.where(hit, NEG_INF, s)
    return (vals, rank) if want_rank else vals


def _peer_select(q_scr, keys_ref, c_scr, e1_scr, r2_scr, e2_scr, head):
    row0 = pl.multiple_of(head * PEER_DKEY, PEER_DKEY)
    half = PEER_DKEY // 2
    s1_all = jnp.dot(keys_ref[0], q_scr[pl.ds(row0, half), :],
                     precision=lax.Precision.HIGHEST, preferred_element_type=F32)
    s2_all = jnp.dot(keys_ref[1], q_scr[pl.ds(row0 + half, half), :],
                     precision=lax.Precision.HIGHEST, preferred_element_type=F32)
    width = min(LANES, s1_all.shape[1])
    for g in range(s1_all.shape[1] // width):
        cols = slice(g * width, (g + 1) * width)
        _peer_select_group(s1_all[:, cols], s2_all[:, cols], c_scr, e1_scr, r2_scr, e2_scr, head, cols)


def _peer_select_group(s1, s2, c_scr, e1_scr, r2_scr, e2_scr, head, cols):
    a = _top_values(s1, PEER_TOPK)
    b, rank2 = _top_values(s2, PEER_TOPK, want_rank=True)
    s1 = s1 - a[0]
    s2 = s2 - b[0]
    a = [v - a[0] for v in a]
    b = [v - b[0] for v in b]
    cand = [a[i] + b[j] for i in range(PEER_TOPK) for j in range(PEER_TOPK) if (i + 1) * (j + 1) <= PEER_TOPK]
    pad = (-len(cand)) % 8
    cand = jnp.concatenate(cand + [jnp.full_like(cand[0], NEG_INF)] * pad, axis=0)
    top = _top_values(cand, PEER_TOPK)
    cut = top[-1]
    z = jnp.zeros_like(cut)
    for v in top:
        z = z + jnp.exp(v)
    count = jnp.zeros_like(s1)
    for bj in b:
        count = count + jnp.where(s1 + bj >= cut, 1.0, 0.0)
    c_scr[head, :, cols] = count
    e1_scr[head, :, cols] = jnp.exp(s1) / z
    r2_scr[head, :, cols] = rank2.astype(BF16)
    e2_scr[head, :, cols] = jnp.exp(s2).astype(BF16)


def _peer_kernel(x_ref, sc_ref, sh_ref, gate_ref, wq_ref, keys_ref, u_ref, vt_ref, lng_ref, lnb_ref,
                 o_ref, ht_scr, q_scr, c_scr, e1_scr, r2_scr, e2_scr, acc_scr, a_scr, p_scr, *, te, eb):
    j = pl.program_id(1)

    @pl.when(j == 0)
    def _():
        h = x_ref[...] * (1.0 + sc_ref[0]) + sh_ref[0]
        ht_scr[...] = jnp.transpose(h).astype(BF16)
        q_scr[...] = jnp.dot(wq_ref[...], ht_scr[...], preferred_element_type=F32)

        def head_body(head, carry):
            _peer_select(q_scr, keys_ref, c_scr, e1_scr, r2_scr, e2_scr, head)
            return carry
        lax.fori_loop(0, PEER_HEADS, head_body, 0, unroll=2)
        acc_scr[...] = jnp.zeros_like(acc_scr)

    def activations(e):
        a_scr[e % 3] = jnp.dot(u_ref[e * eb:(e + 1) * eb, :], ht_scr[...],
                               preferred_element_type=F32)

    def weigh(e):
        key1 = (j * te + e * eb) // PEER_NKEYS
        for i1 in range(eb // PEER_NKEYS):
            rows = slice(i1 * PEER_NKEYS, (i1 + 1) * PEER_NKEYS)
            w = None
            for head in range(PEER_HEADS):
                row = pl.ds(key1 + i1, 1)
                keep = r2_scr[head] < c_scr[head, row, :].astype(BF16)
                val = jnp.where(keep, e2_scr[head] * e1_scr[head, row, :].astype(BF16), jnp.zeros((), BF16))
                w = val if w is None else w + val
            a_half = a_scr[e % 3, rows, :]
            gelu = a_half * (1.0 + lax.erf(a_half * (2.0 ** 0.5)))
            p_scr[e % 2, rows, :] = w * gelu.astype(BF16)

    def combine(e):
        acc_scr[...] += jnp.dot(vt_ref[:, e * eb:(e + 1) * eb], p_scr[e % 2],
                                preferred_element_type=F32)

    n_blocks = te // eb
    activations(0)
    activations(1)
    for e in range(n_blocks):
        if e + 2 < n_blocks:
            activations(e + 2)
        if e > 0:
            combine(e - 1)
        weigh(e)
    combine(n_blocks - 1)

    @pl.when(j == pl.num_programs(1) - 1)
    def _():
        out = jnp.transpose(acc_scr[...])
        z = DN_ALPHA * x_ref[...] + (1.0 + gate_ref[0]) * out
        o_ref[...] = _layernorm_rows(z, lng_ref[...], lnb_ref[...])


def _peer_layer(x, mods, wq_t, keys, u_half, v_t, ln_g, ln_b, seq_len, tm_want=512, te=2048, eb=512):
    shift, scale, gate = mods
    n, d = x.shape
    tm = _row_tile(n, tm_want)
    n_exp = u_half.shape[0]
    sc, sc_blk, sc_idx = _mod_operand(scale, seq_len, tm)
    sh, sh_blk, sh_idx = _mod_operand(shift, seq_len, tm)
    gt, gt_blk, gt_idx = _mod_operand(gate, seq_len, tm)
    hq = PEER_HEADS * PEER_DKEY
    row = lambda f: (lambda i, j: f(i))
    return pl.pallas_call(
        functools.partial(_peer_kernel, te=te, eb=eb),
        grid=(n // tm, n_exp // te),
        in_specs=[pl.BlockSpec((tm, d), lambda i, j: (i, 0)),
                  pl.BlockSpec(sc_blk, row(sc_idx)),
                  pl.BlockSpec(sh_blk, row(sh_idx)),
                  pl.BlockSpec(gt_blk, row(gt_idx)),
                  pl.BlockSpec((hq, d), lambda i, j: (0, 0)),
                  pl.BlockSpec((2, PEER_NKEYS, PEER_DKEY // 2), lambda i, j: (0, 0, 0)),
                  pl.BlockSpec((te, d), lambda i, j: (j, 0)),
                  pl.BlockSpec((d, te), lambda i, j: (0, j)),
                  pl.BlockSpec((1, d), lambda i, j: (0, 0)),
                  pl.BlockSpec((1, d), lambda i, j: (0, 0))],
        out_specs=pl.BlockSpec((tm, d), lambda i, j: (i, 0)),
        out_shape=jax.ShapeDtypeStruct((n, d), F32),
        scratch_shapes=[pltpu.VMEM((d, tm), BF16),
                        pltpu.VMEM((hq, tm), F32),
                        pltpu.VMEM((PEER_HEADS, PEER_NKEYS, tm), F32),
                        pltpu.VMEM((PEER_HEADS, PEER_NKEYS, tm), F32),
                        pltpu.VMEM((PEER_HEADS, PEER_NKEYS, tm), BF16),
                        pltpu.VMEM((PEER_HEADS, PEER_NKEYS, tm), BF16),
                        pltpu.VMEM((d, tm), F32),
                        pltpu.VMEM((3, eb, tm), F32),
                        pltpu.VMEM((2, eb, tm), BF16)],
        compiler_params=_cparams("parallel", "arbitrary"),
        name="peer",
    )(x, sc, sh, gt, wq_t, keys, u_half, v_t, ln_g.reshape(1, d), ln_b.reshape(1, d))


def _block_mean_kernel(*refs, n_parts):
    o_ref = refs[-1]
    total = None
    rows = 0
    for p_ref in refs[-1 - n_parts:-1]:
        part = jnp.sum(p_ref[0].astype(F32), axis=0, keepdims=True)
        total = part if total is None else total + part
        rows += p_ref.shape[1]
    o_ref[0] = total * (1.0 / rows)


def _block_means_dense(k, batch, seq_len):
    w = k.shape[1]
    nb = seq_len // MOBA_BLOCK
    out = pl.pallas_call(
        functools.partial(_block_mean_kernel, n_parts=1),
        grid=(batch * nb,),
        in_specs=[pl.BlockSpec((1, MOBA_BLOCK, w), lambda i: (i, 0, 0))],
        out_specs=pl.BlockSpec((1, 1, w), lambda i: (i, 0, 0)),
        out_shape=jax.ShapeDtypeStruct((batch * nb, 1, w), F32),
        compiler_params=_cparams("parallel"),
        name="block_means",
    )(k.reshape(batch * nb, MOBA_BLOCK, w))
    return out.reshape(batch, nb, w)


def _moba_prompt_kernel(qi_ref, kv0_ref, kv1_ref, q_ref, k0_ref, v0_ref, k1_ref, v1_ref, mean_ref, o_ref,
                        m_scr, acc_scr, qa_scr):
    t = pl.program_id(1)
    qi = qi_ref[t]
    kv0 = kv0_ref[t]
    kv1 = kv1_ref[t]
    own = kv0 == qi
    blk = MOBA_BLOCK
    masked = 1e30
    row = lax.broadcasted_iota(jnp.int32, (blk, blk), 0)
    col = lax.broadcasted_iota(jnp.int32, (blk, blk), 1)
    causal_bias = jnp.where(own, jnp.where(col <= row, 0.0, NEG_INF), 0.0)
    gcol_i = lax.broadcasted_iota(jnp.int32, (blk, LANES), 1)
    head_cols = [slice(h * HEAD_DIM, (h + 1) * HEAD_DIM) for h in range(MOBA_HEADS)]

    @pl.when(own)
    def _():
        n_rows = mean_ref.shape[1]
        grow = lax.broadcasted_iota(jnp.int32, (n_rows, blk), 0).astype(F32)
        n_past = qi.astype(F32)
        for h, cs in enumerate(head_cols):
            g = lax.dot_general(mean_ref[0, :, cs], q_ref[0, :, cs].astype(F32), (((1,), (1,)), ((), ())),
                                precision=lax.Precision.HIGHEST, preferred_element_type=F32)
            g = jnp.where(grow < n_past, g, NEG_INF)
            bias_t = jnp.where(grow == n_past, 0.0, -masked)
            for r in range(MOBA_TOPK):
                best = jnp.max(g, axis=0, keepdims=True)
                pick = jnp.min(jnp.where(g == best, grow, float(n_rows)), axis=0, keepdims=True)
                hit = grow == pick
                bias_t = jnp.where(r < qi, jnp.where(hit, 0.0, bias_t), bias_t)
                g = jnp.where(hit, NEG_INF, g)
            if n_rows < LANES:
                bias_t = jnp.concatenate([bias_t, jnp.full((LANES - n_rows, blk), -masked, F32)], axis=0)
            qa_scr[h] = jnp.concatenate([q_ref[0, :, cs], jnp.transpose(bias_t).astype(BF16)], axis=1)
            m_scr[h] = jnp.full((blk, LANES), NEG_INF, F32)
            acc_scr[h] = jnp.zeros((blk, 2 * HEAD_DIM), F32)

    onehot0 = jnp.where(gcol_i == kv0, 1.0, 0.0).astype(BF16)
    onehot1 = jnp.where(gcol_i == kv1, 1.0, 0.0).astype(BF16)
    absent1 = jnp.where(kv1 < 0, -masked, 0.0)
    ones = jnp.ones((blk, HEAD_DIM), BF16)
    twice = lambda a: jnp.concatenate([a, a], axis=1)
    nt = (((1,), (1,)), ((), ()))
    for h, cs in enumerate(head_cols):
        qa = qa_scr[h]
        s0 = lax.dot_general(qa, jnp.concatenate([k0_ref[0, :, cs], onehot0], axis=1), nt,
                             preferred_element_type=F32) + causal_bias
        s1 = lax.dot_general(qa, jnp.concatenate([k1_ref[0, :, cs], onehot1], axis=1), nt,
                             preferred_element_type=F32) + absent1
        m_old = m_scr[h]
        m_new = jnp.maximum(m_old, jnp.maximum(jnp.max(s0, axis=-1, keepdims=True),
                                               jnp.max(s1, axis=-1, keepdims=True)))
        alpha = jnp.exp(m_old - m_new)
        p0 = jnp.exp(s0 - twice(m_new)).astype(BF16)
        p1 = jnp.exp(s1 - twice(m_new)).astype(BF16)
        acc_scr[h] = (twice(alpha) * acc_scr[h]
                      + jnp.dot(p0, jnp.concatenate([v0_ref[0, :, cs], ones], axis=1), preferred_element_type=F32)
                      + jnp.dot(p1, jnp.concatenate([v1_ref[0, :, cs], ones], axis=1), preferred_element_type=F32))
        m_scr[h] = m_new

    @pl.when(kv0 <= 1)
    def _():
        for h, cs in enumerate(head_cols):
            acc = acc_scr[h]
            o_ref[0, :, cs] = (acc[:, :HEAD_DIM] / acc[:, HEAD_DIM:]).astype(o_ref.dtype)


def _moba_prompt(q, k, v, means, batch, seq_len):
    w = q.shape[1]
    nb = seq_len // MOBA_BLOCK
    assert nb <= LANES
    n_rows = -(-nb // 8) * 8
    means = jnp.pad(means, ((0, 0), (0, n_rows - nb), (0, 0)))
    steps = [(qi, kv, kv - 1) for qi in range(nb) for kv in range(qi, -1, -2)]
    qi_tab, kv0_tab, kv1_tab = (jnp.asarray(col, jnp.int32) for col in zip(*steps))
    r3 = lambda a: a.reshape(batch, seq_len, w)
    q_spec = pl.BlockSpec((1, MOBA_BLOCK, w), lambda b, t, qt, k0, k1: (b, qt[t], 0))
    kv0_spec = pl.BlockSpec((1, MOBA_BLOCK, w), lambda b, t, qt, k0, k1: (b, k0[t], 0))
    kv1_spec = pl.BlockSpec((1, MOBA_BLOCK, w), lambda b, t, qt, k0, k1: (b, jnp.maximum(k1[t], 0), 0))
    out = pl.pallas_call(
        _moba_prompt_kernel,
        grid_spec=pltpu.PrefetchScalarGridSpec(
            num_scalar_prefetch=3,
            grid=(batch, len(steps)),
            in_specs=[q_spec, kv0_spec, kv0_spec, kv1_spec, kv1_spec,
                      pl.BlockSpec((1, n_rows, w), lambda b, t, qt, k0, k1: (b, 0, 0))],
            out_specs=q_spec,
            scratch_shapes=[pltpu.VMEM((MOBA_HEADS, MOBA_BLOCK, LANES), F32),
                            pltpu.VMEM((MOBA_HEADS, MOBA_BLOCK, 2 * HEAD_DIM), F32),
                            pltpu.VMEM((MOBA_HEADS, MOBA_BLOCK, HEAD_DIM + LANES), BF16)]),
        out_shape=jax.ShapeDtypeStruct((batch, seq_len, w), BF16),
        compiler_params=_cparams("parallel", "arbitrary"),
        name="moba_prompt",
    )(qi_tab, kv0_tab, kv1_tab, r3(q), r3(k), r3(v), r3(k), r3(v), means)
    return out.reshape(batch * seq_len, w)


def _moba_decode_kernel(pt_ref, q_ref, kn_ref, vn_ref, *rest, n_new, n_past, ppb):
    k_pages = rest[:ppb]
    v_pages = rest[ppb:2 * ppb]
    o_ref, g_scr, m_scr, l_scr, acc_scr = rest[2 * ppb:]
    j = pl.program_id(1)
    nq = MOBA_HEADS * n_new
    head_bits = MOBA_HEADS.bit_length() - 1
    new_bits = n_new.bit_length() - 1
    q = q_ref[0]
    qb = q.astype(BF16)

    def same_head(n_cols):
        lane_head = lax.broadcasted_iota(jnp.int32, (nq, n_cols), 1) & (MOBA_HEADS - 1)
        row_head = lax.broadcasted_iota(jnp.int32, (nq, n_cols), 0) >> new_bits
        return lane_head == row_head

    def scores(keys):
        return lax.dot_general(qb, keys, (((1,), (1,)), ((), ())), preferred_element_type=F32)

    page_len = k_pages[0].shape[1]
    page_rows = page_len * MOBA_HEADS
    pages_per_block = MOBA_BLOCK // page_len
    head_bias = jnp.where(same_head(page_rows), 0.0, NEG_INF)
    for bi in range(ppb // pages_per_block):
        block = j * (ppb // pages_per_block) + bi
        k_blk = k_pages[bi * pages_per_block:(bi + 1) * pages_per_block]
        v_blk = v_pages[bi * pages_per_block:(bi + 1) * pages_per_block]
        key_sum = None
        for k_ref in k_blk:
            part = jnp.sum(k_ref[0], axis=0)
            key_sum = part if key_sum is None else key_sum + part
        gate = lax.dot_general(q, key_sum * (1.0 / MOBA_BLOCK), (((1,), (1,)), ((), ())),
                               precision=lax.Precision.HIGHEST, preferred_element_type=F32)
        g_scr[block] = jnp.sum(jnp.where(same_head(MOBA_HEADS), gate, 0.0), axis=-1, keepdims=True)
        s_pages = [scores(k_ref[0].reshape(page_rows, HEAD_DIM).astype(BF16)) + head_bias for k_ref in k_blk]
        m = None
        for s in s_pages:
            s_max = jnp.max(s, axis=-1, keepdims=True)
            m = s_max if m is None else jnp.maximum(m, s_max)
        l = None
        acc = None
        for s, v_ref in zip(s_pages, v_blk):
            p = jnp.exp(s - m)
            p_sum = jnp.sum(p, axis=-1, keepdims=True)
            pv = jnp.dot(p.astype(BF16), v_ref[0].reshape(page_rows, HEAD_DIM).astype(BF16),
                         preferred_element_type=F32)
            l = p_sum if l is None else l + p_sum
            acc = pv if acc is None else acc + pv
        m_scr[block] = m
        l_scr[block] = l
        acc_scr[block] = acc

    @pl.when(j == pl.num_programs(1) - 1)
    def _():
        s = scores(kn_ref[0].astype(BF16))
        key_t = lax.broadcasted_iota(jnp.int32, s.shape, 1) >> head_bits
        qry_r = lax.broadcasted_iota(jnp.int32, s.shape, 0) & (n_new - 1)
        s = jnp.where(same_head(s.shape[1]), jnp.where(key_t <= qry_r, s, NEG_INF), NEG_INF)
        m_own = jnp.max(s, axis=-1, keepdims=True)
        p = jnp.exp(s - m_own)
        l_own = jnp.sum(p, axis=-1, keepdims=True)
        acc_own = jnp.dot(p.astype(BF16), vn_ref[0].astype(BF16), preferred_element_type=F32)

        k_sel = min(MOBA_TOPK, n_past)
        gates = [g_scr[b] for b in range(n_past)]
        weights = []
        m_all = m_own
        for b in range(n_past):
            place = jnp.zeros_like(gates[b])
            for o in range(n_past):
                if o != b:
                    ahead = (gates[o] >= gates[b]) if o < b else (gates[o] > gates[b])
                    place = place + jnp.where(ahead, 1.0, 0.0)
            weights.append(jnp.where(place < k_sel, 0.0, NEG_INF))
            m_all = jnp.maximum(m_all, m_scr[b] + weights[b])
        w_own = jnp.exp(m_own - m_all)
        l_all = w_own * l_own
        acc_all = w_own * acc_own
        for b in range(n_past):
            w_b = jnp.exp(m_scr[b] + weights[b] - m_all)
            l_all = l_all + w_b * l_scr[b]
            acc_all = acc_all + w_b * acc_scr[b]
        o_ref[0] = acc_all / l_all


def _moba_decode(q, k_new, v_new, cache_k, cache_v, page_table_flat, batch, n_new, pages_per_seq):
    _, page, heads, hd = cache_k.shape
    assert heads & (heads - 1) == 0 and n_new & (n_new - 1) == 0
    n_past = pages_per_seq * page // MOBA_BLOCK
    blocks_per_step = next(c for c in (4, 2, 1) if n_past % c == 0)
    ppb = (MOBA_BLOCK // page) * blocks_per_step
    nq = heads * n_new
    q_hq = q.reshape(batch, n_new, heads, hd).transpose(0, 2, 1, 3).reshape(batch, nq, hd)
    kv_rows = lambda a: a.reshape(batch, nq, hd)
    seq_spec = pl.BlockSpec((1, nq, hd), lambda b, j, pt: (b, 0, 0))
    page_spec = lambda part: pl.BlockSpec(
        (1, page, heads, hd), lambda b, j, pt: (pt[b * pages_per_seq + j * ppb + part], 0, 0, 0))
    out = pl.pallas_call(
        functools.partial(_moba_decode_kernel, n_new=n_new, n_past=n_past, ppb=ppb),
        grid_spec=pltpu.PrefetchScalarGridSpec(
            num_scalar_prefetch=1,
            grid=(batch, n_past // blocks_per_step),
            in_specs=[seq_spec, seq_spec, seq_spec] + [page_spec(p) for p in range(ppb)] * 2,
            out_specs=seq_spec,
            scratch_shapes=[pltpu.VMEM((n_past, nq, 1), F32),
                            pltpu.VMEM((n_past, nq, 1), F32),
                            pltpu.VMEM((n_past, nq, 1), F32),
                            pltpu.VMEM((n_past, nq, hd), F32)]),
        out_shape=jax.ShapeDtypeStruct((batch, nq, hd), F32),
        compiler_params=_cparams("parallel", "arbitrary"),
        name="moba_decode",
    )(page_table_flat, q_hq, kv_rows(k_new), kv_rows(v_new),
      *([cache_k] * ppb), *([cache_v] * ppb))
    return out.reshape(batch, heads, n_new, hd).transpose(0, 2, 1, 3).reshape(batch * n_new, heads * hd)


def _rope_tables(pos):
    half = HEAD_DIM // 2
    inv = ROPE_THETA ** (-jnp.arange(half, dtype=F32) / half)
    ang = pos.astype(F32)[:, None] * inv[None, :]
    cos, sin = jnp.cos(ang), jnp.sin(ang)
    return jnp.concatenate([cos, cos], axis=-1), jnp.concatenate([-sin, sin], axis=-1)


def _trunk(x, mods, mods_kv, s0, past, p, batch, seq_len):
    pos0 = 0 if past is None else past[3] * past[0].shape[1]
    pos = pos0 + jnp.arange(seq_len, dtype=jnp.int32)
    cos, sin = _rope_tables(pos)
    n = batch * seq_len
    if seq_len % _row_tile(n, 512):
        cos, sin = jnp.tile(cos, (batch, 1)), jnp.tile(sin, (batch, 1))
    gla_fin = []
    k_new = v_new = None
    dm = MOBA_HEADS * HEAD_DIM
    for layer in range(DEPTH):
        shift, scale, gate = mods[layer][0]
        if layer < N_A:
            x, s_fin = _gla_layer(x, mods[layer][0], s0[layer], p["gla_w_in"][layer], p["gla_w_g2"][layer],
                                  p["gla_b_g"][layer], p["gla_norm_g"][layer], p["gla_w_o"][layer],
                                  p["ln_g"][layer, 0], p["ln_b"][layer, 0], batch, seq_len)
            gla_fin.append(s_fin)
        else:
            jb = layer - N_A
            (q,) = _proj(x, scale, shift, p["moba_w_q"][jb], seq_len, [(BF16 if past is None else F32, 0, dm)],
                         rope=(cos, sin), rope_heads=MOBA_HEADS, out_scale=HEAD_DIM ** -0.5)
            if past is None:
                o = _moba_prompt(q, k_bf16, v_bf16, means, batch, seq_len)
            else:
                o = _moba_decode(q, k_new, v_new, past[0], past[1], past[2], batch, seq_len,
                                 past[3]).astype(BF16)
            x = _out_ln(o, p["moba_w_o"][jb], x, gate, p["ln_g"][layer, 0], p["ln_b"][layer, 0], seq_len)
        x = _peer_layer(x, mods[layer][1], p["peer_w_q"][layer], p["peer_keys"][layer], p["peer_u"][layer],
                        p["peer_v"][layer], p["ln_g"][layer, 1], p["ln_b"][layer, 1], seq_len)
        if layer == N_A - 1:
            kv_shift, kv_scale = mods_kv
            kv_cols = [(F32, 0, dm), (F32, dm, 2 * dm)]
            if past is None:
                kv_cols += [(BF16, 0, dm), (BF16, dm, 2 * dm)]
            kv = _proj(x, kv_scale, kv_shift, p["w_kv"], seq_len, kv_cols, rope=(cos, sin),
                       rope_heads=MOBA_HEADS)
            k_new, v_new = kv[:2]
            if past is None:
                k_bf16, v_bf16 = kv[2:]
                means = _block_means_dense(k_new, batch, seq_len)
    return x, jnp.stack(gla_fin), k_new, v_new


def kernel(x_prompt, x_sample, c_prompt, c_sample, state_gla, cache_k, cache_v, page_table,
           w_ada, b_ada, ln_g, ln_b, gla_w_in, gla_w_g2, gla_b_g, gla_norm_g, gla_w_o,
           w_ada_kv, b_ada_kv, w_kv, moba_w_q, moba_w_o, peer_w_q, peer_keys, peer_u, peer_v):
    b_p, l_p, d = x_prompt.shape
    b_s, l_s, _ = x_sample.shape
    gla_cols = -(-gla_w_in.shape[-1] // LANES) * LANES
    p = {
        "ln_g": ln_g, "ln_b": ln_b,
        "gla_w_in": jnp.pad(gla_w_in, ((0, 0), (0, 0), (0, gla_cols - gla_w_in.shape[-1]))).astype(BF16),
        "gla_w_g2": jnp.pad(gla_w_g2, ((0, 0), (0, LANES - GLA_GATE_RANK), (0, 0))),
        "gla_b_g": gla_b_g, "gla_norm_g": gla_norm_g,
        "gla_w_o": gla_w_o.astype(BF16),
        "w_kv": w_kv.astype(BF16),
        "moba_w_q": moba_w_q.astype(BF16), "moba_w_o": moba_w_o.astype(BF16),
        "peer_w_q": jnp.swapaxes(peer_w_q, 1, 2).astype(BF16),
        "peer_keys": peer_keys,
        "peer_u": (0.5 * peer_u).astype(BF16),
        "peer_v": jnp.swapaxes(peer_v, 1, 2).astype(BF16),
    }
    c_all = jnp.concatenate([c_prompt, c_sample], axis=0)
    rows = c_all.shape[0]
    c_all = jnp.pad(c_all, ((0, (-rows) % 8), (0, 0)))
    m_all = _ada_all(c_all, w_ada.reshape(DEPTH * 2, d, 3 * d), b_ada.reshape(DEPTH * 2, 1, 3 * d))
    m_kv = _ada_all(c_all, w_ada_kv[None], b_ada_kv[None, None])[0]

    def group_mods(lo, hi):
        mods = [[tuple(m_all[layer * 2 + sub, lo:hi, i * d:(i + 1) * d] for i in range(3))
                 for sub in range(2)] for layer in range(DEPTH)]
        return mods, (m_kv[lo:hi, :d], m_kv[lo:hi, d:])

    mods_p, kv_p = group_mods(0, b_p)
    mods_s, kv_s = group_mods(b_p, b_p + b_s)

    zero_state = jnp.zeros((N_A, b_p) + state_gla.shape[2:], F32)
    y_p, gla_p, k_p, v_p = _trunk(x_prompt.reshape(b_p * l_p, d), mods_p, kv_p, zero_state, None, p, b_p, l_p)

    past = (cache_k, cache_v, page_table.reshape(-1).astype(jnp.int32), page_table.shape[1])
    y_s, gla_s, k_s, v_s = _trunk(x_sample.reshape(b_s * l_s, d), mods_s, kv_s, state_gla, past, p, b_s, l_s)

    heads = lambda a, b, l: a.reshape(b, l, MOBA_HEADS, HEAD_DIM)
    return (y_p.reshape(b_p, l_p, d), y_s.reshape(b_s, l_s, d), gla_p, gla_s,
            heads(k_p, b_p, l_p), heads(v_p, b_p, l_p), heads(k_s, b_s, l_s), heads(v_s, b_s, l_s))
```
